```python
import math
import jax
import jax.numpy as jnp
from jax import lax
import numpy as np

D_MODEL = 1024
BATCH = 32
SEQ = 2048
DEPTH = 4

GRID_W = 64
CTX_LEN = 256
N_MIXERS = 4
EPS = 1e-6
ROPE_THETA = 10000.0
Q_BLOCK = 128
NEG_INF = -1e30

MLA_HEADS = 16
MLA_NOPE = 64
MLA_ROPE = 32
MLA_QK = MLA_NOPE + MLA_ROPE
MLA_V = 64
MLA_Q_LORA = 384
MLA_KV_LORA = 256

S5_GROUP = 16
S5_GROUPS = D_MODEL // S5_GROUP
S5_STATE = 64
S5_DT_MIN = 1e-3
S5_DT_MAX = 1e-1

NA_HEADS = 16
NA_HEAD_DIM = D_MODEL // NA_HEADS
NA_WIN_H = 8
NA_WIN_W = 16

GQA_HEADS = 8
GQA_KV_HEADS = 2
GQA_HEAD_DIM = D_MODEL // GQA_HEADS

FFN_HIDDEN = -(-(8 * D_MODEL) // (3 * 256)) * 256

kernel_name = 'hybrid_interleaved_mla_s5_natten_gqa_dit'


def rmsnorm(x, g):
    xf = x.astype(jnp.float32)
    y = xf * lax.rsqrt(jnp.mean(xf * xf, axis=-1, keepdims=True) + EPS)
    return (y * g.astype(jnp.float32)).astype(x.dtype)


def modulate(x, g, shift, scale):
    return rmsnorm(x, g) * (1 + scale) + shift


def ada_terms(cvec, w, b):
    m = jax.nn.silu(cvec) @ w + b
    return jnp.split(m[..., None, :], 6, axis=-1)


def swiglu(h, w_in, w_out):
    a, b = jnp.split(h @ w_in, 2, axis=-1)
    return (jax.nn.silu(a) * b) @ w_out


def axial_rope_tables(n_tokens, rot_dim):
    t = jnp.arange(n_tokens, dtype=jnp.int32)
    axis_dim = rot_dim // 2
    inv_freq = ROPE_THETA ** (-jnp.arange(0, axis_dim, 2, dtype=jnp.float32) / axis_dim)

    def table(pos):
        ang = pos.astype(jnp.float32)[:, None] * inv_freq[None, :]
        return jnp.cos(ang), jnp.sin(ang)

    return table(t // GRID_W), table(t % GRID_W)


def rotate_pairs(x, cos, sin):
    d2 = x.shape[-1] // 2
    x1, x2 = x[..., :d2], x[..., d2:]
    c, s = cos[:, None, :], sin[:, None, :]
    return jnp.concatenate([x1 * c - x2 * s, x1 * s + x2 * c], axis=-1).astype(x.dtype)


def apply_axial_rope(x, tables):
    (cos_r, sin_r), (cos_c, sin_c) = tables
    half = x.shape[-1] // 2
    return jnp.concatenate([rotate_pairs(x[..., :half], cos_r, sin_r),
                            rotate_pairs(x[..., half:], cos_c, sin_c)], axis=-1)


def ctx_attention(q, k, v):
    B, C, H, dk = q.shape
    Hk = k.shape[2]
    qg = q.reshape(B, C, Hk, H // Hk, dk)
    s = jnp.einsum('bqkgd,bnkd->bkgqn', qg, k).astype(jnp.float32) * (dk ** -0.5)
    p = jax.nn.softmax(s, axis=-1).astype(v.dtype)
    o = jnp.einsum('bkgqn,bnkd->bqkgd', p, v)
    return o.reshape(B, C, H, v.shape[-1])


def latent_attention(q, kc, vc, kl, vl):
    B, S, H, dk = q.shape
    Hk = kl.shape[2]
    G = H // Hk
    dv = vl.shape[-1]
    k = jnp.concatenate([kc, kl], axis=1)
    v = jnp.concatenate([vc, vl], axis=1)
    nb = S // Q_BLOCK
    qb = q.reshape(B, nb, Q_BLOCK, Hk, G, dk).transpose(1, 0, 2, 3, 4, 5)
    scale = dk ** -0.5

    def one_block(qi):
        s = jnp.einsum('bqkgd,bnkd->bkgqn', qi, k).astype(jnp.float32) * scale
        p = jax.nn.softmax(s, axis=-1).astype(v.dtype)
        return jnp.einsum('bkgqn,bnkd->bqkgd', p, v)

    o = lax.map(one_block, qb)
    return o.transpose(1, 0, 2, 3, 4, 5).reshape(B, S, H, dv)


def qk_normed_qkv(h, w_qkv, n_q, n_kv, dh, g_qn, g_kn, tables, need_q):
    B, L, _ = h.shape
    q_cols = n_q * dh
    if need_q:
        z = h @ w_qkv
        q = rmsnorm(z[..., :q_cols].reshape(B, L, n_q, dh), g_qn)
        kv = z[..., q_cols:]
    else:
        q = None
        kv = h @ w_qkv[:, q_cols:]
    k = rmsnorm(kv[..., :n_kv * dh].reshape(B, L, n_kv, dh), g_kn)
    v = kv[..., n_kv * dh:].reshape(B, L, n_kv, dh)
    if tables is not None:
        k = apply_axial_rope(k, tables)
        if need_q:
            q = apply_axial_rope(q, tables)
    return q, k, v


def mla_queries(h, w_in, g_q, w_uq, g_qn, tables):
    B, L, _ = h.shape
    cq = rmsnorm(h @ w_in[:, :MLA_Q_LORA], g_q)
    q = rmsnorm((cq @ w_uq).reshape(B, L, MLA_HEADS, MLA_QK), g_qn)
    if tables is not None:
        q = jnp.concatenate([q[..., :MLA_NOPE], apply_axial_rope(q[..., MLA_NOPE:], tables)], axis=-1)
    return q


def mla_keys_values(h, w_in, g_kv, w_ukv, g_kn, tables):
    B, L, _ = h.shape
    z = h @ w_in[:, MLA_Q_LORA:]
    ckv = rmsnorm(z[..., :MLA_KV_LORA], g_kv)
    k_rope = jnp.broadcast_to(z[..., None, MLA_KV_LORA:], (B, L, MLA_HEADS, MLA_ROPE))
    kv = (ckv @ w_ukv).reshape(B, L, MLA_HEADS, MLA_NOPE + MLA_V)
    k = rmsnorm(jnp.concatenate([kv[..., :MLA_NOPE], k_rope], axis=-1), g_kn)
    if tables is not None:
        k = jnp.concatenate([k[..., :MLA_NOPE], apply_axial_rope(k[..., MLA_NOPE:], tables)], axis=-1)
    return k, kv[..., MLA_NOPE:]


def mla_mixer(hc, hl, w_in, g_q, g_kv, w_uq, w_ukv, g_qn, g_kn, w_o, tables, ctx_out):
    B, S, _ = hl.shape
    kc, vc = mla_keys_values(hc, w_in, g_kv, w_ukv, g_kn, None)
    kl, vl = mla_keys_values(hl, w_in, g_kv, w_ukv, g_kn, tables)
    ql = mla_queries(hl, w_in, g_q, w_uq, g_qn, tables)
    yl = latent_attention(ql, kc, vc, kl, vl).reshape(B, S, MLA_HEADS * MLA_V) @ w_o
    yc = None
    if ctx_out:
        qc = mla_queries(hc, w_in, g_q, w_uq, g_qn, None)
        yc = ctx_attention(qc, kc, vc).reshape(B, hc.shape[1], MLA_HEADS * MLA_V) @ w_o
    return yc, yl


def s5_discretise(a_re, a_im, log_dt, b_re, b_im):
    f32 = jnp.float32
    a_re, a_im = a_re.astype(f32), a_im.astype(f32)
    b_re, b_im = b_re.astype(f32), b_im.astype(f32)
    dt = jnp.exp(log_dt.astype(f32))[:, None]
    mag = jnp.exp(dt * a_re)
    ab_re = mag * jnp.cos(dt * a_im)
    ab_im = mag * jnp.sin(dt * a_im)
    den = a_re * a_re + a_im * a_im
    nr = ab_re - 1.0
    f_re = (nr * a_re + ab_im * a_im) / den
    f_im = (ab_im * a_re - nr * a_im) / den
    bb_re = f_re[..., None] * b_re - f_im[..., None] * b_im
    bb_im = f_re[..., None] * b_im + f_im[..., None] * b_re
    return ab_re, ab_im, bb_re, bb_im


def complex_affine_combine(e1, e2):
    a1r, a1i, b1r, b1i = e1
    a2r, a2i, b2r, b2i = e2
    return (a2r * a1r - a2i * a1i, a2r * a1i + a2i * a1r,
            a2r * b1r - a2i * b1i + b2r, a2r * b1i + a2i * b1r + b2i)


def s5_scan(ab_re, ab_im, bu_re, bu_im, reverse):
    L = bu_re.shape[1]
    a_re = jnp.broadcast_to(ab_re, (1, L) + ab_re.shape)
    a_im = jnp.broadcast_to(ab_im, (1, L) + ab_im.shape)
    _, _, h_re, h_im = lax.associative_scan(complex_affine_combine, (a_re, a_im, bu_re, bu_im),
                                            reverse=reverse, axis=1)
    return h_re, h_im


def s5_drive(u, bb_re, bb_im):
    return (jnp.einsum('blgc,gpc->blgp', u, bb_re), jnp.einsum('blgc,gpc->blgp', u, bb_im))


def s5_readout(h_re, h_im, c_re, c_im):
    return jnp.einsum('blgp,gcp->blgc', h_re, c_re) - jnp.einsum('blgp,gcp->blgc', h_im, c_im)


def s5_glu(y, w_glu):
    g = jax.nn.gelu(y)
    a, b = jnp.split(g @ w_glu, 2, axis=-1)
    return a * jax.nn.sigmoid(b)


def s5_mixer(hc, hl, a_re, a_im, log_dt, b_re, b_im, c_re, c_im, d_skip, w_glu, ctx_out):
    f32 = jnp.float32
    B, S, D = hl.shape
    C = hc.shape[1]
    uc = hc.astype(f32).reshape(B, C, S5_GROUPS, S5_GROUP)
    ul = hl.astype(f32).reshape(B, S, S5_GROUPS, S5_GROUP)
    yl = d_skip.astype(f32) * hl.astype(f32)
    yc = d_skip.astype(f32) * hc.astype(f32) if ctx_out else None
    for direction in range(2):
        reverse = direction == 1
        ab_re, ab_im, bb_re, bb_im = s5_discretise(a_re[direction], a_im[direction], log_dt[direction],
                                                   b_re[direction], b_im[direction])
        cr, ci = c_re[direction].astype(f32), c_im[direction].astype(f32)
        bc_re, bc_im = s5_drive(uc, bb_re, bb_im)
        sc_re, sc_im = s5_scan(ab_re, ab_im, bc_re, bc_im, reverse)
        edge_c = 0 if reverse else C - 1
        h0_re, h0_im = sc_re[:, edge_c], sc_im[:, edge_c]
        bl_re, bl_im = s5_drive(ul, bb_re, bb_im)
        edge_l = S - 1 if reverse else 0
        bl_re = bl_re.at[:, edge_l].add(ab_re * h0_re - ab_im * h0_im)
        bl_im = bl_im.at[:, edge_l].add(ab_re * h0_im + ab_im * h0_re)
        sl_re, sl_im = s5_scan(ab_re, ab_im, bl_re, bl_im, reverse)
        yl = yl + s5_readout(sl_re, sl_im, cr, ci).reshape(B, S, D)
        if ctx_out:
            yc = yc + s5_readout(sc_re, sc_im, cr, ci).reshape(B, C, D)
    out_l = s5_glu(yl.astype(hl.dtype), w_glu)
    out_c = s5_glu(yc.astype(hc.dtype), w_glu) if ctx_out else None
    return out_c, out_l


def na_mixer(hc, hl, w_qkv, g_qn, g_kn, rpb, w_o, ctx_out):
    B, S, _ = hl.shape
    rows = S // GRID_W
    kh = min(NA_WIN_H, rows)
    qc, kc, vc = qk_normed_qkv(hc, w_qkv, NA_HEADS, NA_HEADS, NA_HEAD_DIM, g_qn, g_kn, None, ctx_out)
    ql, kl, vl = qk_normed_qkv(hl, w_qkv, NA_HEADS, NA_HEADS, NA_HEAD_DIM, g_qn, g_kn, None, True)
    grid = (B, rows, GRID_W, NA_HEADS, NA_HEAD_DIM)
    q_grid, k_grid, v_grid = ql.reshape(grid), kl.reshape(grid), vl.reshape(grid)
    j = jnp.arange(GRID_W)
    col_start = jnp.clip(j - NA_WIN_W // 2, 0, GRID_W - NA_WIN_W)
    col_ok = (j[None, :] >= col_start[:, None]) & (j[None, :] < col_start[:, None] + NA_WIN_W)
    col_idx = jnp.clip(j[None, :] - j[:, None] + NA_WIN_W - 1, 0, 2 * NA_WIN_W - 2)
    key_ok = jnp.broadcast_to(col_ok[:, None, :], (GRID_W, kh, GRID_W)).reshape(GRID_W, kh * GRID_W)
    scale = NA_HEAD_DIM ** -0.5
    n_ctx = kc.shape[1]

    def one_row(i):
        r0 = jnp.clip(i - kh // 2, 0, rows - kh)
        kb = lax.dynamic_slice_in_dim(k_grid, r0, kh, axis=1).reshape(B, kh * GRID_W, NA_HEADS, NA_HEAD_DIM)
        vb = lax.dynamic_slice_in_dim(v_grid, r0, kh, axis=1).reshape(B, kh * GRID_W, NA_HEADS, NA_HEAD_DIM)
        qi = lax.dynamic_index_in_dim(q_grid, i, axis=1, keepdims=False)
        row_idx = r0 + jnp.arange(kh) - i + NA_WIN_H - 1
        bias = rpb[:, row_idx][:, :, col_idx]
        bias = bias.transpose(0, 2, 1, 3).reshape(NA_HEADS, GRID_W, kh * GRID_W).astype(jnp.float32)
        s_lat = jnp.einsum('bqhd,bkhd->bhqk', qi, kb).astype(jnp.float32) * scale + bias
        s_lat = jnp.where(key_ok, s_lat, NEG_INF)
        s_ctx = jnp.einsum('bqhd,bkhd->bhqk', qi, kc).astype(jnp.float32) * scale
        p = jax.nn.softmax(jnp.concatenate([s_ctx, s_lat], axis=-1), axis=-1).astype(vb.dtype)
        return (jnp.einsum('bhqk,bkhd->bqhd', p[..., :n_ctx], vc)
                + jnp.einsum('bhqk,bkhd->bqhd', p[..., n_ctx:], vb))

    o = lax.map(one_row, jnp.arange(rows))
    yl = o.transpose(1, 0, 2, 3, 4).reshape(B, S, NA_HEADS * NA_HEAD_DIM) @ w_o
    yc = None
    if ctx_out:
        yc = ctx_attention(qc, kc, vc).reshape(B, n_ctx, NA_HEADS * NA_HEAD_DIM) @ w_o
    return yc, yl


def gqa_mixer(hc, hl, w_qkv, g_qn, g_kn, w_o, tables, ctx_out):
    B, S, _ = hl.shape
    qc, kc, vc = qk_normed_qkv(hc, w_qkv, GQA_HEADS, GQA_KV_HEADS, GQA_HEAD_DIM, g_qn, g_kn, None, ctx_out)
    ql, kl, vl = qk_normed_qkv(hl, w_qkv, GQA_HEADS, GQA_KV_HEADS, GQA_HEAD_DIM, g_qn, g_kn, tables, True)
    yl = latent_attention(ql, kc, vc, kl, vl).reshape(B, S, GQA_HEADS * GQA_HEAD_DIM) @ w_o
    yc = None
    if ctx_out:
        yc = ctx_attention(qc, kc, vc).reshape(B, hc.shape[1], GQA_HEADS * GQA_HEAD_DIM) @ w_o
    return yc, yl


def setup_inputs(seed: int = 0) -> dict:
    key = jax.random.key(seed)
    keys = jax.random.split(key, 64)
    counter = iter(range(64))
    f32 = jnp.float32
    D = D_MODEL
    G, P, CG = S5_GROUPS, S5_STATE, S5_GROUP
    nA, nB, nC, nD = (len(range(k, DEPTH, N_MIXERS)) for k in range(N_MIXERS))

    def nrm(shape, scale):
        return scale * jax.random.normal(keys[next(counter)], shape, f32)

    def gain(shape):
        return 1.0 + nrm(shape, 0.05)

    n_idx = jnp.arange(S5_STATE, dtype=f32)
    return {
        'x': nrm((BATCH, SEQ, D), 1.0),
        'c': nrm((BATCH, D), 1.0),
        'ctx': nrm((BATCH, CTX_LEN, D), 1.0),
        'c_ctx': nrm((D,), 1.0),
        'ada_w': nrm((DEPTH, D, 6 * D), 0.5 * D ** -0.5),
        'ada_b': nrm((DEPTH, 6 * D), 0.02),
        'norm_mix': gain((DEPTH, D)),
        'norm_ffn': gain((DEPTH, D)),
        'ffn_w_in': nrm((DEPTH, D, 2 * FFN_HIDDEN), D ** -0.5),
        'ffn_w_out': nrm((DEPTH, FFN_HIDDEN, D), FFN_HIDDEN ** -0.5),
        'mla_w_in': nrm((nA, D, MLA_Q_LORA + MLA_KV_LORA + MLA_ROPE), D ** -0.5),
        'mla_g_q': gain((nA, MLA_Q_LORA)),
        'mla_g_kv': gain((nA, MLA_KV_LORA)),
        'mla_w_uq': nrm((nA, MLA_Q_LORA, MLA_HEADS * MLA_QK), MLA_Q_LORA ** -0.5),
        'mla_w_ukv': nrm((nA, MLA_KV_LORA, MLA_HEADS * (MLA_NOPE + MLA_V)), MLA_KV_LORA ** -0.5),
        'mla_g_qn': gain((nA, MLA_QK)),
        'mla_g_kn': gain((nA, MLA_QK)),
        'mla_w_o': nrm((nA, MLA_HEADS * MLA_V, D), (MLA_HEADS * MLA_V) ** -0.5),
        's5_a_re': -0.5 + nrm((nB, 2, G, P), 0.01),
        's5_a_im': jnp.pi * n_idx + nrm((nB, 2, G, P), 0.01),
        's5_log_dt': jax.random.uniform(keys[next(counter)], (nB, 2, G), f32,
                                        math.log(S5_DT_MIN), math.log(S5_DT_MAX)),
        's5_b_re': nrm((nB, 2, G, P, CG), (2 * CG) ** -0.5),
        's5_b_im': nrm((nB, 2, G, P, CG), (2 * CG) ** -0.5),
        's5_c_re': nrm((nB, 2, G, CG, P), P ** -0.5),
        's5_c_im': nrm((nB, 2, G, CG, P), P ** -0.5),
        's5_d': nrm((nB, D), 0.5),
        's5_w_glu': nrm((nB, D, 2 * D), D ** -0.5),
        'na_w_qkv': nrm((nC, D, 3 * NA_HEADS * NA_HEAD_DIM), D ** -0.5),
        'na_g_qn': gain((nC, NA_HEAD_DIM)),
        'na_g_kn': gain((nC, NA_HEAD_DIM)),
        'na_rpb': nrm((nC, NA_HEADS, 2 * NA_WIN_H - 1, 2 * NA_WIN_W - 1), 0.1),
        'na_w_o': nrm((nC, NA_HEADS * NA_HEAD_DIM, D), (NA_HEADS * NA_HEAD_DIM) ** -0.5),
        'gqa_w_qkv': nrm((nD, D, (GQA_HEADS + 2 * GQA_KV_HEADS) * GQA_HEAD_DIM), D ** -0.5),
        'gqa_g_qn': gain((nD, GQA_HEAD_DIM)),
        'gqa_g_kn': gain((nD, GQA_HEAD_DIM)),
        'gqa_w_o': nrm((nD, GQA_HEADS * GQA_HEAD_DIM, D), (GQA_HEADS * GQA_HEAD_DIM) ** -0.5),
    }


def reference(x, c, ctx, c_ctx, ada_w, ada_b, norm_mix, norm_ffn, ffn_w_in, ffn_w_out,
              mla_w_in, mla_g_q, mla_g_kv, mla_w_uq, mla_w_ukv, mla_g_qn, mla_g_kn, mla_w_o,
              s5_a_re, s5_a_im, s5_log_dt, s5_b_re, s5_b_im, s5_c_re, s5_c_im, s5_d, s5_w_glu,
              na_w_qkv, na_g_qn, na_g_kn, na_rpb, na_w_o,
              gqa_w_qkv, gqa_g_qn, gqa_g_kn, gqa_w_o):
    S = x.shape[1]
    mla_tables = axial_rope_tables(S, MLA_ROPE)
    gqa_tables = axial_rope_tables(S, GQA_HEAD_DIM)
    xl, xc = x, ctx
    for i in range(DEPTH):
        kind, j = i % N_MIXERS, i // N_MIXERS
        ctx_out = i < DEPTH - 1
        sh_l, sc_l, gt_l, sh2_l, sc2_l, gt2_l = ada_terms(c, ada_w[i], ada_b[i])
        sh_c, sc_c, gt_c, sh2_c, sc2_c, gt2_c = ada_terms(c_ctx, ada_w[i], ada_b[i])
        hl = modulate(xl, norm_mix[i], sh_l, sc_l)
        hc = modulate(xc, norm_mix[i], sh_c, sc_c)
        if kind == 0:
            yc, yl = mla_mixer(hc, hl, mla_w_in[j], mla_g_q[j], mla_g_kv[j], mla_w_uq[j], mla_w_ukv[j],
                               mla_g_qn[j], mla_g_kn[j], mla_w_o[j], mla_tables, ctx_out)
        elif kind == 1:
            yc, yl = s5_mixer(hc, hl, s5_a_re[j], s5_a_im[j], s5_log_dt[j], s5_b_re[j], s5_b_im[j],
                              s5_c_re[j], s5_c_im[j], s5_d[j], s5_w_glu[j], ctx_out)
        elif kind == 2:
            yc, yl = na_mixer(hc, hl, na_w_qkv[j], na_g_qn[j], na_g_kn[j], na_rpb[j], na_w_o[j], ctx_out)
        else:
            yc, yl = gqa_mixer(hc, hl, gqa_w_qkv[j], gqa_g_qn[j], gqa_g_kn[j], gqa_w_o[j], gqa_tables, ctx_out)
        xl = xl + gt_l * yl
        xl = xl + gt2_l * swiglu(modulate(xl, norm_ffn[i], sh2_l, sc2_l), ffn_w_in[i], ffn_w_out[i])
        if ctx_out:
            xc = xc + gt_c * yc
            xc = xc + gt2_c * swiglu(modulate(xc, norm_ffn[i], sh2_c, sc2_c), ffn_w_in[i], ffn_w_out[i])
    return xl
```

```python
import functools
import math

import jax
import jax.numpy as jnp
from jax import lax
from jax.experimental import pallas as pl
from jax.experimental.pallas import tpu as pltpu

F32 = jnp.float32
BF16 = jnp.bfloat16

EPS = 1e-6
ROPE_THETA = 10000.0
GRID_W = 64
NEG_INF = -1e30

MLA_HEADS = 16
MLA_NOPE = 64
MLA_ROPE = 32
MLA_QK = MLA_NOPE + MLA_ROPE
MLA_V = 64
MLA_Q_LORA = 384
MLA_KV_LORA = 256
MLA_SLOT = 128

S5_GROUP = 16
S5_STATE = 64
S5_CHUNK = 16

NA_HEADS = 16
NA_WIN_H = 8
NA_WIN_W = 16
NA_QROWS = 4
NA_BAND = NA_QROWS + NA_WIN_H

GQA_HEADS = 8
GQA_KV_HEADS = 2

LANES = 128
VMEM_LIMIT = 56 * 1024 * 1024


def _cparams(*sem):
    return pltpu.CompilerParams(dimension_semantics=sem, vmem_limit_bytes=VMEM_LIMIT)


def _modulate(x, g, shift, scale):
    ms = jnp.mean(x * x, axis=-1, keepdims=True)
    return x * lax.rsqrt(ms + EPS) * g * (1.0 + scale) + shift


def _rms_rows(x, n):
    ss = jnp.sum(x * x, axis=-1, keepdims=True)
    return lax.rsqrt(ss * (1.0 / n) + EPS)


def _silu(x):
    return x * jax.nn.sigmoid(x)


def _gelu_tanh(x):
    return 0.5 * x * (1.0 + jnp.tanh(math.sqrt(2.0 / math.pi) * (x + 0.044715 * x * x * x)))


def _rope_lanes(x, cos, sin_signed, lower, shift):
    width = x.shape[-1]
    partner = jnp.where(lower, pltpu.roll(x, width - shift, 1), pltpu.roll(x, shift, 1))
    return x * cos + partner * sin_signed


def _dot(a, b):
    return jnp.dot(a, b, preferred_element_type=F32)


def _dot_nt(a, b):
    return lax.dot_general(a, b, (((1,), (1,)), ((), ())), preferred_element_type=F32)


def _ada_kernel(c_ref, w_ref, b_ref, o_ref):
    c = c_ref[...]
    o_ref[...] = _dot(_silu(c), w_ref[...]) + b_ref[...]


def _ada_terms(cvec, ada_w, ada_b):
    depth, d, n6 = ada_w.shape
    rows = cvec.shape[0]
    tn = 1536
    return pl.pallas_call(
        _ada_kernel,
        out_shape=jax.ShapeDtypeStruct((depth, rows, n6), F32),
        grid=(depth, n6 // tn),
        in_specs=[
            pl.BlockSpec((rows, d), lambda i, j: (0, 0)),
            pl.BlockSpec((None, d, tn), lambda i, j: (i, 0, j)),
            pl.BlockSpec((None, 1, tn), lambda i, j: (i, 0, j)),
        ],
        out_specs=pl.BlockSpec((None, rows, tn), lambda i, j: (i, 0, j)),
        compiler_params=_cparams("parallel", "parallel"),
        name="ada_terms",
    )(cvec, ada_w, ada_b.reshape(depth, 1, n6))


class _Mod:
    def __init__(self, table, d, ctx_row):
        self.table, self.d, self.ctx_row = table, d, ctx_row

    def spec(self, term, is_ctx, batch_axis):
        ctx_row = self.ctx_row
        if is_ctx:
            return pl.BlockSpec((None, 1, self.d), lambda *ids: (ctx_row, 0, term))
        return pl.BlockSpec((None, 1, self.d), lambda *ids: (ids[batch_axis], 0, term))


def _const_spec(shape):
    nd = len(shape)
    return pl.BlockSpec(shape, lambda *ids: (0,) * nd)


def _tok_spec(tm, width):
    return pl.BlockSpec((None, tm, width), lambda b, j: (b, j, 0))


def _mla_proj_kernel(x_ref, sh_ref, sc_ref, g_ref, cos_ref, sin_ref, w_in_ref, gq_ref, gkv_ref,
                     w_uq_ref, w_ukv_ref, gqn_ref, gkn_ref, gkr_ref, gkrs_ref,
                     q_ref, k_ref, v_ref, *, use_rope, need_q):
    h = _modulate(x_ref[...], g_ref[...], sh_ref[...], sc_ref[...]).astype(BF16)
    z = _dot(h, w_in_ref[...])
    lane = lax.broadcasted_iota(jnp.int32, (1, MLA_SLOT), 1)
    lower = ((lane - MLA_NOPE) % (MLA_ROPE // 2)) < (MLA_ROPE // 4)
    if use_rope:
        cos, sin = cos_ref[...], sin_ref[...]

    kv0 = MLA_Q_LORA
    zkv = z[:, kv0:kv0 + MLA_KV_LORA]
    ckv = (zkv * _rms_rows(zkv, MLA_KV_LORA) * gkv_ref[...]).astype(BF16)
    kv = _dot(ckv, w_ukv_ref[...])
    v_ref[...] = kv[:, MLA_HEADS * MLA_SLOT:].astype(BF16)
    r0 = kv0 + MLA_KV_LORA
    zr = z[:, r0:r0 + MLA_SLOT]
    zrs = z[:, r0 + MLA_SLOT:r0 + 2 * MLA_SLOT]
    ss_rope = jnp.sum(zr * zr, axis=-1, keepdims=True)
    if use_rope:
        kr = zr * gkr_ref[...] * cos + zrs * gkrs_ref[...] * sin
    else:
        kr = zr * gkr_ref[...]
    gkn = gkn_ref[...]
    for hd in range(MLA_HEADS):
        seg = kv[:, hd * MLA_SLOT:(hd + 1) * MLA_SLOT]
        ss = jnp.sum(seg * seg, axis=-1, keepdims=True) + ss_rope
        r = lax.rsqrt(ss * (1.0 / MLA_QK) + EPS)
        k_ref[:, hd * MLA_SLOT:(hd + 1) * MLA_SLOT] = ((seg * gkn + kr) * r).astype(BF16)

    if need_q:
        zq = z[:, :MLA_Q_LORA]
        cq = (zq * _rms_rows(zq, MLA_Q_LORA) * gq_ref[...]).astype(BF16)
        qp = _dot(cq, w_uq_ref[...])
        gqn = gqn_ref[...] * (MLA_QK ** -0.5)
        for hd in range(MLA_HEADS):
            seg = qp[:, hd * MLA_SLOT:(hd + 1) * MLA_SLOT]
            qn = seg * _rms_rows(seg, MLA_QK) * gqn
            if use_rope:
                qn = _rope_lanes(qn, cos, sin, lower, MLA_ROPE // 4)
            q_ref[:, hd * MLA_SLOT:(hd + 1) * MLA_SLOT] = qn.astype(BF16)
    else:
        q_ref[...] = jnp.zeros_like(q_ref)


def _mla_proj(x, mod, is_ctx, norm_g, tabs, wts, tm, need_q=True):
    b, l, d = x.shape
    use_rope = not is_ctx
    cos, sin = tabs
    qw = MLA_HEADS * MLA_SLOT
    tab_spec = pl.BlockSpec((tm, MLA_SLOT), lambda bb, j: (j, 0))
    in_specs = [_tok_spec(tm, d), mod.spec(0, is_ctx, 0), mod.spec(1, is_ctx, 0), _const_spec((1, d)),
                tab_spec, tab_spec] + [_const_spec(w.shape) for w in wts]
    out_q = qw if need_q else LANES
    return pl.pallas_call(
        functools.partial(_mla_proj_kernel, use_rope=use_rope, need_q=need_q),
        out_shape=(jax.ShapeDtypeStruct((b, l, out_q), BF16),
                   jax.ShapeDtypeStruct((b, l, qw), BF16),
                   jax.ShapeDtypeStruct((b, l, MLA_HEADS * MLA_V), BF16)),
        grid=(b, l // tm),
        in_specs=in_specs,
        out_specs=(_tok_spec(tm, out_q), _tok_spec(tm, qw), _tok_spec(tm, MLA_HEADS * MLA_V)),
        compiler_params=_cparams("parallel", "parallel"),
        name="mla_proj_ctx" if is_ctx else "mla_proj",
    )(x, mod.table, mod.table, norm_g, cos, sin, *wts)


def _qkv_proj_kernel(x_ref, sh_ref, sc_ref, g_ref, cos_ref, sin_ref, w_ref, gq_ref, gk_ref,
                     q_ref, k_ref, v_ref, *, n_q, n_kv, dh, use_rope, need_q, q_scale):
    h = _modulate(x_ref[...], g_ref[...], sh_ref[...], sc_ref[...]).astype(BF16)
    z = _dot(h, w_ref[...])
    q_cols, kv_cols = n_q * dh, n_kv * dh
    v_ref[...] = z[:, q_cols + kv_cols:].astype(BF16)
    lane = lax.broadcasted_iota(jnp.int32, (1, LANES), 1)
    if use_rope:
        cos, sin = cos_ref[...], sin_ref[...]
        lower = (lane % (dh // 2)) < (dh // 4)

    def normed(seg, gain):
        sq = seg * seg
        if dh == LANES:
            r = lax.rsqrt(jnp.sum(sq, axis=-1, keepdims=True) * (1.0 / dh) + EPS)
        else:
            first = lane < dh
            s_all = jnp.sum(sq, axis=-1, keepdims=True)
            s_first = jnp.sum(jnp.where(first, sq, 0.0), axis=-1, keepdims=True)
            r = lax.rsqrt(jnp.where(first, s_first, s_all - s_first) * (1.0 / dh) + EPS)
        out = seg * r * gain
        if use_rope:
            out = _rope_lanes(out, cos, sin, lower, dh // 4)
        return out.astype(BF16)

    gk = gk_ref[...]
    for s in range(kv_cols // LANES):
        k_ref[:, s * LANES:(s + 1) * LANES] = normed(z[:, q_cols + s * LANES:q_cols + (s + 1) * LANES], gk)
    if need_q:
        gq = gq_ref[...] * q_scale
        for s in range(q_cols // LANES):
            q_ref[:, s * LANES:(s + 1) * LANES] = normed(z[:, s * LANES:(s + 1) * LANES], gq)
    else:
        q_ref[...] = jnp.zeros_like(q_ref)


def _qkv_proj(x, mod, is_ctx, norm_g, tabs, w, gq, gk, n_q, n_kv, dh, tm, use_rope, need_q, name):
    b, l, d = x.shape
    cos, sin = tabs
    q_cols, kv_cols = n_q * dh, n_kv * dh
    if not need_q:
        w = w[:, q_cols:]
    tab_spec = pl.BlockSpec((tm, LANES), lambda bb, j: (j, 0))
    out_q = q_cols if need_q else LANES
    kern = functools.partial(_qkv_proj_kernel, n_q=n_q if need_q else 0, n_kv=n_kv, dh=dh,
                             use_rope=use_rope, need_q=need_q, q_scale=dh ** -0.5)
    return pl.pallas_call(
        kern,
        out_shape=(jax.ShapeDtypeStruct((b, l, out_q), BF16),
                   jax.ShapeDtypeStruct((b, l, kv_cols), BF16),
                   jax.ShapeDtypeStruct((b, l, kv_cols), BF16)),
        grid=(b, l // tm),
        in_specs=[_tok_spec(tm, d), mod.spec(0, is_ctx, 0), mod.spec(1, is_ctx, 0), _const_spec((1, d)),
                  tab_spec, tab_spec, _const_spec(w.shape), _const_spec(gq.shape), _const_spec(gk.shape)],
        out_specs=(_tok_spec(tm, out_q), _tok_spec(tm, kv_cols), _tok_spec(tm, kv_cols)),
        compiler_params=_cparams("parallel", "parallel"),
        name=name,
    )(x, mod.table, mod.table, norm_g, cos, sin, w, gq, gk)


def _softmax_pv(s_parts, v_parts):
    m = s_parts[0].max(axis=-1, keepdims=True)
    for s in s_parts[1:]:
        m = jnp.maximum(m, s.max(axis=-1, keepdims=True))
    denom = None
    acc = None
    for s, v in zip(s_parts, v_parts):
        p = jnp.exp(s - m)
        d = jnp.sum(p, axis=-1, keepdims=True)
        o = _dot(p.astype(BF16), v)
        denom = d if denom is None else denom + d
        acc = o if acc is None else acc + o
    return acc * (1.0 / denom)


def _attn_kernel(*refs, tq, heads, k_slot, pair_v, has_latent):
    if has_latent:
        q_ref, kc_ref, vc_ref, kl_ref, vl_ref, o_ref = refs
    else:
        q_ref, kc_ref, vc_ref, o_ref = refs
        kl_ref = vl_ref = None
    n_tiles = q_ref.shape[0] // tq
    lane = lax.broadcasted_iota(jnp.int32, (1, LANES), 1)
    first = lane < (LANES // 2)

    def tile(i, carry):
        rows = pl.ds(pl.multiple_of(i * tq, tq), tq)
        outs = []
        for e in range(heads):
            if k_slot is None:
                q = q_ref[rows, :]
                q = jnp.where(first if e == 0 else jnp.logical_not(first), q, jnp.zeros_like(q))
                ksl = slice(0, LANES)
            else:
                q = q_ref[rows, e * LANES:(e + 1) * LANES]
                ksl = slice(e * LANES, (e + 1) * LANES) if k_slot else slice(0, LANES)
            s_parts = [_dot_nt(q, kc_ref[:, ksl])]
            v_parts = [vc_ref[...]]
            if has_latent:
                s_parts.append(_dot_nt(q, kl_ref[:, ksl]))
                v_parts.append(vl_ref[...])
            outs.append(_softmax_pv(s_parts, v_parts))
        if pair_v:
            o_ref[rows, :] = jnp.where(first, outs[0], outs[1]).astype(o_ref.dtype)
        else:
            for e in range(heads):
                o_ref[rows, e * LANES:(e + 1) * LANES] = outs[e].astype(o_ref.dtype)
        return carry

    lax.fori_loop(0, n_tiles, tile, 0)


def _attention(q, kc, vc, kl, vl, *, groups, q_lanes, k_lanes, heads, k_slot, pair_v, tq, name):
    b, l, _ = q.shape
    c = kc.shape[1]
    has_latent = kl is not None
    o_lanes = LANES if pair_v else heads * LANES

    def gspec(n, w):
        return pl.BlockSpec((None, n, w), lambda bb, g: (bb, 0, g))

    in_specs = [gspec(l, q_lanes), gspec(c, k_lanes), gspec(c, LANES)]
    args = [q, kc, vc]
    if has_latent:
        s = kl.shape[1]
        in_specs += [gspec(s, k_lanes), gspec(s, LANES)]
        args += [kl, vl]
    return pl.pallas_call(
        functools.partial(_attn_kernel, tq=tq, heads=heads, k_slot=k_slot, pair_v=pair_v,
                          has_latent=has_latent),
        out_shape=jax.ShapeDtypeStruct((b, l, groups * o_lanes), BF16),
        grid=(b, groups),
        in_specs=in_specs,
        out_specs=gspec(l, o_lanes),
        compiler_params=_cparams("parallel", "parallel"),
        name=name,
    )(*args)


def _na_kernel(q_ref, kc_ref, vc_ref, kl_ref, vl_ref, bias_ref, o_ref, *, n_blocks):
    tq = NA_QROWS * GRID_W
    band = NA_BAND * GRID_W
    rows_total = n_blocks * NA_QROWS
    lane = lax.broadcasted_iota(jnp.int32, (1, LANES), 1)
    first = lane < (LANES // 2)

    def block(i, carry):
        rows = pl.ds(pl.multiple_of(i * tq, tq), tq)
        band_row0 = jnp.clip(i * NA_QROWS - NA_WIN_H // 2, 0, rows_total - NA_BAND)
        keys = pl.ds(pl.multiple_of(band_row0 * GRID_W, GRID_W), band)
        variant = jnp.where(i == 0, 0, jnp.where(i == n_blocks - 1, 2, 1))
        q2 = q_ref[rows, :]
        kb, vb = kl_ref[keys, :], vl_ref[keys, :]
        outs = []
        for e in range(2):
            q = jnp.where(first if e == 0 else jnp.logical_not(first), q2, jnp.zeros_like(q2))
            s_ctx = _dot_nt(q, kc_ref[...])
            s_lat = _dot_nt(q, kb) + bias_ref[e, variant]
            outs.append(_softmax_pv([s_ctx, s_lat], [vc_ref[...], vb]))
        o_ref[rows, :] = jnp.where(first, outs[0], outs[1]).astype(o_ref.dtype)
        return carry

    lax.fori_loop(0, n_blocks, block, 0)


def _na_attention(q, kc, vc, kl, vl, bias):
    b, s, w = q.shape
    c = kc.shape[1]
    pairs = w // LANES
    n_blocks = s // (NA_QROWS * GRID_W)

    def gspec(n):
        return pl.BlockSpec((None, n, LANES), lambda g, bb: (bb, 0, g))

    return pl.pallas_call(
        functools.partial(_na_kernel, n_blocks=n_blocks),
        out_shape=jax.ShapeDtypeStruct((b, s, w), BF16),
        grid=(pairs, b),
        in_specs=[gspec(s), gspec(c), gspec(c), gspec(s), gspec(s),
                  pl.BlockSpec((2,) + bias.shape[1:], lambda g, bb: (g, 0, 0, 0))],
        out_specs=gspec(s),
        compiler_params=_cparams("parallel", "parallel"),
        name="na_attention",
    )(q, kc, vc, kl, vl, bias)


def _na_bias_table(rpb, rows):
    n_blocks = rows // NA_QROWS
    tables = []
    for i in (0, 1, n_blocks - 1):
        band0 = min(max(i * NA_QROWS - NA_WIN_H // 2, 0), rows - NA_BAND)
        qr = i * NA_QROWS + jnp.arange(NA_QROWS)
        kr = band0 + jnp.arange(NA_BAND)
        r0 = jnp.clip(qr - NA_WIN_H // 2, 0, rows - NA_WIN_H)
        row_ok = (kr[None, :] >= r0[:, None]) & (kr[None, :] < r0[:, None] + NA_WIN_H)
        row_idx = jnp.clip(kr[None, :] - qr[:, None] + NA_WIN_H - 1, 0, 2 * NA_WIN_H - 2)
        j = jnp.arange(GRID_W)
        c0 = jnp.clip(j - NA_WIN_W // 2, 0, GRID_W - NA_WIN_W)
        col_ok = (j[None, :] >= c0[:, None]) & (j[None, :] < c0[:, None] + NA_WIN_W)
        col_idx = jnp.clip(j[None, :] - j[:, None] + NA_WIN_W - 1, 0, 2 * NA_WIN_W - 2)
        bias = rpb[:, row_idx][:, :, :, col_idx]
        ok = row_ok[:, :, None, None] & col_ok[None, None, :, :]
        bias = jnp.where(ok[None], bias.astype(F32), NEG_INF)
        tables.append(bias.transpose(0, 1, 3, 2, 4).reshape(rpb.shape[0], NA_QROWS * GRID_W, NA_BAND * GRID_W))
    return jnp.stack(tables, axis=1)


def _s5_kernel(uc_ref, ul_ref, kc_ref, ws_ref, wo_ref, dec_ref, yc_ref, yl_ref, s_ref, h_ref, *, batch):
    half = 2 * S5_STATE
    lane = lax.broadcasted_iota(jnp.int32, (1, half), 1)
    fwd = lane < S5_STATE
    dec = dec_ref[...]
    d_re, d_im = dec[:, :half], dec[:, half:]

    def scan(u_ref, y_ref, n_chunks, init_re, init_im):
        rows = n_chunks * batch
        u = u_ref[...]
        s_ref[0:rows, :] = _dot(u, ws_ref[...])

        def step(i, carry):
            st_re, st_im = carry
            rf = pl.ds(pl.multiple_of(i * batch, batch), batch)
            rb = pl.ds(pl.multiple_of((n_chunks - 1 - i) * batch, batch), batch)
            h_ref[rf, 0:S5_STATE] = st_re[:, :S5_STATE]
            h_ref[rf, half:half + S5_STATE] = st_im[:, :S5_STATE]
            h_ref[rb, S5_STATE:half] = st_re[:, S5_STATE:]
            h_ref[rb, half + S5_STATE:] = st_im[:, S5_STATE:]
            in_re = jnp.where(fwd, s_ref[rf, 0:half], s_ref[rb, 0:half])
            in_im = jnp.where(fwd, s_ref[rf, half:], s_ref[rb, half:])
            return (d_re * st_re - d_im * st_im + in_re, d_re * st_im + d_im * st_re + in_im)

        st_re, st_im = lax.fori_loop(0, n_chunks, step, (init_re, init_im))
        y = _dot(u, kc_ref[...]) + _dot(h_ref[0:rows, :].astype(BF16), wo_ref[...])
        y_ref[...] = y.astype(y_ref.dtype)
        return st_re, st_im

    zero = jnp.zeros((batch, half), F32)
    ctx_re, ctx_im = scan(uc_ref, yc_ref, uc_ref.shape[0] // batch, zero, zero)
    scan(ul_ref, yl_ref, ul_ref.shape[0] // batch, ctx_re, ctx_im)


def _s5_scan(uc, ul, kc, ws, wo, dec, batch):
    groups, rows_c, width = uc.shape
    rows_l = ul.shape[1]

    def gspec(n, w):
        return pl.BlockSpec((None, n, w), lambda g: (g, 0, 0))

    return pl.pallas_call(
        functools.partial(_s5_kernel, batch=batch),
        out_shape=(jax.ShapeDtypeStruct((groups, rows_c, width), BF16),
                   jax.ShapeDtypeStruct((groups, rows_l, width), BF16)),
        grid=(groups,),
        in_specs=[gspec(rows_c, width), gspec(rows_l, width), gspec(width, width), gspec(width, width),
                  gspec(width, width), gspec(1, width)],
        out_specs=(gspec(rows_c, width), gspec(rows_l, width)),
        scratch_shapes=[pltpu.VMEM((rows_l, width), F32), pltpu.VMEM((rows_l, width), F32)],
        compiler_params=_cparams("parallel"),
        name="s5_scan",
    )(uc, ul, kc, ws, wo, dec)


def _s5_matrices(a_re, a_im, log_dt, b_re, b_im, c_re, c_im):
    t = S5_CHUNK
    dt = jnp.exp(log_dt.astype(F32))[..., None]
    a_re, a_im = a_re.astype(F32), a_im.astype(F32)
    lam_re, lam_im = dt * a_re, dt * a_im

    def power(j):
        jj = jnp.asarray(j, F32)
        mag = jnp.exp(lam_re[..., None] * jj)
        return mag * jnp.cos(lam_im[..., None] * jj), mag * jnp.sin(lam_im[..., None] * jj)

    ab_re, ab_im = power(jnp.ones((1,)))
    ab_re, ab_im = ab_re[..., 0], ab_im[..., 0]
    den = a_re * a_re + a_im * a_im
    nr = ab_re - 1.0
    f_re = (nr * a_re + ab_im * a_im) / den
    f_im = (ab_im * a_re - nr * a_im) / den
    bb_re = f_re[..., None] * b_re - f_im[..., None] * b_im
    bb_im = f_re[..., None] * b_im + f_im[..., None] * b_re
    c_re, c_im = c_re.astype(F32), c_im.astype(F32)

    hi = lax.Precision.HIGHEST
    lags = jnp.arange(t)
    pw_re, pw_im = power(lags)
    cr, ci = c_re[:, :, :, None, :], c_im[:, :, :, None, :]
    pr, pi = pw_re.transpose(0, 1, 3, 2)[:, :, None], pw_im.transpose(0, 1, 3, 2)[:, :, None]
    cb_re = cr * pr - ci * pi
    cb_im = cr * pi + ci * pr
    conv = (jnp.einsum('dgclp,dgpi->dglic', cb_re, bb_re, precision=hi)
            - jnp.einsum('dgclp,dgpi->dglic', cb_im, bb_im, precision=hi))
    src, dst = lags[:, None], lags[None, :]
    lag_f = dst - src
    k_f = jnp.where((lag_f >= 0)[None, :, None, :, None], conv[0][:, jnp.clip(lag_f, 0, t - 1)].transpose(0, 1, 3, 2, 4), 0.0)
    lag_b = src - dst
    k_b = jnp.where((lag_b >= 0)[None, :, None, :, None], conv[1][:, jnp.clip(lag_b, 0, t - 1)].transpose(0, 1, 3, 2, 4), 0.0)
    g = a_re.shape[1]
    cg = b_re.shape[-1]
    k_mat = (k_f + k_b).reshape(g, t * cg, t * cg)

    def state_w(d, expo):
        p_re, p_im = power(expo)
        w_re = p_re[d][..., None] * bb_re[d][:, :, None, :] - p_im[d][..., None] * bb_im[d][:, :, None, :]
        w_im = p_re[d][..., None] * bb_im[d][:, :, None, :] + p_im[d][..., None] * bb_re[d][:, :, None, :]
        to_rows = lambda w: w.transpose(0, 2, 3, 1).reshape(g, t * cg, -1)
        return to_rows(w_re), to_rows(w_im)

    wf_re, wf_im = state_w(0, t - 1 - lags)
    wb_re, wb_im = state_w(1, lags)
    w_state = jnp.concatenate([wf_re, wb_re, wf_im, wb_im], axis=-1)

    def out_w(d, expo):
        p_re, p_im = power(expo)
        m_re = c_re[d][:, :, :, None] * p_re[d][:, None, :, :] - c_im[d][:, :, :, None] * p_im[d][:, None, :, :]
        m_im = c_re[d][:, :, :, None] * p_im[d][:, None, :, :] + c_im[d][:, :, :, None] * p_re[d][:, None, :, :]
        to_cols = lambda w: w.transpose(0, 2, 3, 1).reshape(g, -1, t * cg)
        return to_cols(m_re), -to_cols(m_im)

    of_re, of_im = out_w(0, lags + 1)
    ob_re, ob_im = out_w(1, t - lags)
    w_out = jnp.concatenate([of_re, ob_re, of_im, ob_im], axis=1)

    dT_re, dT_im = power(jnp.full((1,), float(t)))
    dT_re, dT_im = dT_re[..., 0], dT_im[..., 0]
    decay = jnp.concatenate([dT_re[0], dT_re[1], dT_im[0], dT_im[1]], axis=-1)[:, None, :]
    return k_mat.astype(BF16), w_state.astype(BF16), w_out.astype(BF16), decay


def _s5_proj_kernel(x_ref, sh_ref, sc_ref, g_ref, u_ref):
    u_ref[...] = _modulate(x_ref[...], g_ref[...], sh_ref[...], sc_ref[...]).astype(BF16)


def _s5_proj(x, mod, is_ctx, norm_g, tm):
    b, l, d = x.shape
    return pl.pallas_call(
        _s5_proj_kernel,
        out_shape=jax.ShapeDtypeStruct((b, l, d), BF16),
        grid=(b, l // tm),
        in_specs=[_tok_spec(tm, d), mod.spec(0, is_ctx, 0), mod.spec(1, is_ctx, 0), _const_spec((1, d))],
        out_specs=_tok_spec(tm, d),
        compiler_params=_cparams("parallel", "parallel"),
        name="s5_proj",
    )(x, mod.table, mod.table, norm_g)


def _to_chunks(u):
    b, l, d = u.shape
    g = d // S5_GROUP
    u = u.reshape(b, l // S5_CHUNK, S5_CHUNK, g, S5_GROUP).transpose(3, 1, 0, 2, 4)
    return u.reshape(g, (l // S5_CHUNK) * b, S5_CHUNK * S5_GROUP)


def _from_chunks(y, b):
    g, rows, _ = y.shape
    n = rows // b
    y = y.reshape(g, n, b, S5_CHUNK, S5_GROUP).transpose(2, 1, 3, 0, 4)
    return y.reshape(b, n * S5_CHUNK, g * S5_GROUP)


def _post_ffn_kernel(x_ref, y_ref, m_ref, gmix_ref, gffn_ref, dskip_ref, w_mix_ref, w_in_ref, w_out_ref,
                     o_ref, acc_ref, *, glu, hidden, chunk):
    sh, sc, gt, sh2, sc2, gt2 = (m_ref[i:i + 1, :] for i in range(6))
    x = x_ref[...]
    if glu:
        yv = dskip_ref[...] * _modulate(x, gmix_ref[...], sh, sc) + y_ref[...].astype(F32)
        ab = _dot(_gelu_tanh(yv).astype(BF16), w_mix_ref[...])
        n = ab.shape[-1] // 2
        mix = ab[:, :n] * jax.nn.sigmoid(ab[:, n:])
    else:
        mix = _dot(y_ref[...], w_mix_ref[...])
    x = x + gt * mix
    h = _modulate(x, gffn_ref[...], sh2, sc2).astype(BF16)
    for c0 in range(0, hidden, chunk):
        a = _dot(h, w_in_ref[:, c0:c0 + chunk])
        b = _dot(h, w_in_ref[:, hidden + c0:hidden + c0 + chunk])
        part = _dot((_silu(a) * b).astype(BF16), w_out_ref[c0:c0 + chunk, :])
        if c0 == 0:
            acc_ref[...] = part
        else:
            acc_ref[...] += part
    o_ref[...] = x + gt2 * acc_ref[...]


def _post_ffn(x, y, mod, is_ctx, g_mix, g_ffn, d_skip, w_mix, w_in, w_out, glu, tm):
    b, l, d = x.shape
    hidden = w_out.shape[0]
    ctx_row = mod.ctx_row
    if is_ctx:
        mspec = pl.BlockSpec((None, 6, d), lambda bb, j: (ctx_row, 0, 0))
    else:
        mspec = pl.BlockSpec((None, 6, d), lambda bb, j: (bb, 0, 0))
    single = dict(pipeline_mode=pl.Buffered(1))

    def wspec(shape):
        nd = len(shape)
        return pl.BlockSpec(shape, lambda *ids: (0,) * nd, **single)

    return pl.pallas_call(
        functools.partial(_post_ffn_kernel, glu=glu, hidden=hidden, chunk=hidden // 2),
        out_shape=jax.ShapeDtypeStruct((b, l, d), F32),
        grid=(b, l // tm),
        in_specs=[_tok_spec(tm, d), _tok_spec(tm, y.shape[-1]), mspec, _const_spec((1, d)), _const_spec((1, d)),
                  _const_spec((1, d)), wspec(w_mix.shape), wspec(w_in.shape), wspec(w_out.shape)],
        out_specs=_tok_spec(tm, d),
        scratch_shapes=[pltpu.VMEM((tm, d), F32)],
        compiler_params=_cparams("parallel", "parallel"),
        name="post_ffn_ctx" if is_ctx else "post_ffn",
    )(x, y, mod.table.reshape(mod.table.shape[0], 6, d), g_mix, g_ffn, d_skip, w_mix, w_in, w_out)


def _rope_tables(n_tokens, rot_dim, offset, width):
    t = jnp.arange(n_tokens, dtype=jnp.int32)
    axis_dim = rot_dim // 2
    inv_freq = ROPE_THETA ** (-jnp.arange(0, axis_dim, 2, dtype=F32) / axis_dim)
    lane = jnp.arange(width)
    j = lane - offset
    in_rot = (j >= 0) & (j < rot_dim)
    jc = jnp.clip(j, 0, rot_dim - 1)
    use_col = jc >= axis_dim
    within = jc % axis_dim
    freq = inv_freq[within % (axis_dim // 2)]
    lower = within < (axis_dim // 2)
    pos = jnp.where(use_col[None, :], (t % GRID_W)[:, None], (t // GRID_W)[:, None]).astype(F32)
    ang = pos * freq[None, :]
    cos = jnp.where(in_rot[None, :], jnp.cos(ang), 1.0)
    sin = jnp.where(in_rot[None, :], jnp.where(lower[None, :], -jnp.sin(ang), jnp.sin(ang)), 0.0)
    return cos, sin


def _mla_weights(w_in, g_q, g_kv, w_uq, w_ukv, g_qn, g_kn):
    d = w_in.shape[0]
    nh, slot = MLA_HEADS, MLA_SLOT
    rope0 = MLA_Q_LORA + MLA_KV_LORA
    w_rope = w_in[:, rope0:]
    quarter = MLA_ROPE // 4
    partner = jnp.arange(MLA_ROPE).reshape(-1, 2, quarter)[:, ::-1, :].reshape(-1)

    def slotted(cols):
        return jnp.pad(cols, ((0, 0), (MLA_NOPE, slot - MLA_QK)))

    w_in_p = jnp.concatenate([w_in[:, :rope0], slotted(w_rope), slotted(w_rope[:, partner])], axis=1)
    w_uq_p = jnp.pad(w_uq.reshape(-1, nh, MLA_QK), ((0, 0), (0, 0), (0, slot - MLA_QK))).reshape(-1, nh * slot)
    kv = w_ukv.reshape(-1, nh, MLA_NOPE + MLA_V)
    w_k = jnp.pad(kv[:, :, :MLA_NOPE], ((0, 0), (0, 0), (0, slot - MLA_NOPE))).reshape(-1, nh * slot)
    w_v = kv[:, :, MLA_NOPE:].reshape(-1, nh * MLA_V)
    w_ukv_p = jnp.concatenate([w_k, w_v], axis=1)
    pad_slot = lambda g: jnp.pad(g, (0, slot - g.shape[0]))[None, :]
    gqn = pad_slot(g_qn)
    gkn_nope = pad_slot(g_kn[:MLA_NOPE])
    gkr = jnp.pad(g_kn[MLA_NOPE:], (MLA_NOPE, slot - MLA_QK))[None, :]
    gkrs = jnp.pad(g_kn[MLA_NOPE:][partner], (MLA_NOPE, slot - MLA_QK))[None, :]
    return (w_in_p.astype(BF16), g_q[None, :], g_kv[None, :], w_uq_p.astype(BF16), w_ukv_p.astype(BF16),
            gqn, gkn_nope, gkr, gkrs)


def _tile_gain(g):
    return jnp.tile(g, LANES // g.shape[0])[None, :]


def kernel(x, c, ctx, c_ctx, ada_w, ada_b, norm_mix, norm_ffn, ffn_w_in, ffn_w_out,
           mla_w_in, mla_g_q, mla_g_kv, mla_w_uq, mla_w_ukv, mla_g_qn, mla_g_kn, mla_w_o,
           s5_a_re, s5_a_im, s5_log_dt, s5_b_re, s5_b_im, s5_c_re, s5_c_im, s5_d, s5_w_glu,
           na_w_qkv, na_g_qn, na_g_kn, na_rpb, na_w_o,
           gqa_w_qkv, gqa_g_qn, gqa_g_kn, gqa_w_o):
    batch, seq, d = x.shape
    n_ctx = ctx.shape[1]
    depth = ada_w.shape[0]
    n_mixers = 4
    ctx_row = batch
    mod_rows = -(-(batch + 1) // 8) * 8
    cvec = jnp.concatenate([c, c_ctx[None, :], jnp.zeros((mod_rows - batch - 1, d), F32)], axis=0)
    ada = _ada_terms(cvec, ada_w, ada_b).reshape(depth, mod_rows, 1, 6 * d)

    mla_tabs = _rope_tables(seq, MLA_ROPE, MLA_NOPE, MLA_SLOT)
    gqa_dh = d // GQA_HEADS
    gqa_tabs = _rope_tables(seq, gqa_dh, 0, LANES)
    ident_tabs = (jnp.ones((n_ctx, LANES), F32), jnp.zeros((n_ctx, LANES), F32))
    ident_tabs_l = (jnp.ones((seq, LANES), F32), jnp.zeros((seq, LANES), F32))
    zero_skip = jnp.zeros((1, d), F32)
    tm_l, tm_c = 512, 256

    xl, xc = x, ctx
    for i in range(depth):
        kind, j = i % n_mixers, i // n_mixers
        ctx_out = i < depth - 1
        mod = _Mod(ada[i], d, ctx_row)
        g_mix, g_ffn = norm_mix[i][None, :], norm_ffn[i][None, :]
        glu = False
        d_skip = zero_skip
        if kind == 0:
            wts = _mla_weights(mla_w_in[j], mla_g_q[j], mla_g_kv[j], mla_w_uq[j], mla_w_ukv[j],
                               mla_g_qn[j], mla_g_kn[j])
            qc, kc, vc = _mla_proj(xc, mod, True, g_mix, ident_tabs, wts, tm_c, need_q=ctx_out)
            ql, kl, vl = _mla_proj(xl, mod, False, g_mix, mla_tabs, wts, tm_c)
            cfg = dict(groups=MLA_HEADS // 2, q_lanes=2 * MLA_SLOT, k_lanes=2 * MLA_SLOT, heads=2,
                       k_slot=True, pair_v=True, tq=256)
            yl = _attention(ql, kc, vc, kl, vl, name="mla_attn", **cfg)
            yc = _attention(qc, kc, vc, None, None, name="mla_attn_ctx", **cfg) if ctx_out else None
            w_mix = mla_w_o[j].astype(BF16)
        elif kind == 1:
            uc = _to_chunks(_s5_proj(xc, mod, True, g_mix, tm_c))
            ul = _to_chunks(_s5_proj(xl, mod, False, g_mix, tm_l))
            mats = _s5_matrices(s5_a_re[j], s5_a_im[j], s5_log_dt[j], s5_b_re[j], s5_b_im[j],
                                s5_c_re[j], s5_c_im[j])
            yc, yl = _s5_scan(uc, ul, *mats, batch)
            yc, yl = _from_chunks(yc, batch), _from_chunks(yl, batch)
            w_mix = s5_w_glu[j].astype(BF16)
            glu = True
            d_skip = s5_d[j][None, :]
        elif kind == 2:
            dh = d // NA_HEADS
            w = na_w_qkv[j].astype(BF16)
            gq, gk = _tile_gain(na_g_qn[j]), _tile_gain(na_g_kn[j])
            qc, kc, vc = _qkv_proj(xc, mod, True, g_mix, ident_tabs, w, gq, gk, NA_HEADS, NA_HEADS, dh, tm_c,
                                   False, ctx_out, "na_proj_ctx")
            ql, kl, vl = _qkv_proj(xl, mod, False, g_mix, ident_tabs_l, w, gq, gk, NA_HEADS, NA_HEADS, dh, tm_l,
                                   False, True, "na_proj")
            bias = _na_bias_table(na_rpb[j], seq // GRID_W)
            yl = _na_attention(ql, kc, vc, kl, vl, bias)
            if ctx_out:
                yc = _attention(qc, kc, vc, None, None, groups=NA_HEADS // 2, q_lanes=LANES, k_lanes=LANES,
                                heads=2, k_slot=None, pair_v=True, tq=256, name="na_attn_ctx")
            w_mix = na_w_o[j].astype(BF16)
        else:
            w = gqa_w_qkv[j].astype(BF16)
            gq, gk = gqa_g_qn[j][None, :], gqa_g_kn[j][None, :]
            per_kv = GQA_HEADS // GQA_KV_HEADS
            qc, kc, vc = _qkv_proj(xc, mod, True, g_mix, ident_tabs, w, gq, gk, GQA_HEADS, GQA_KV_HEADS, gqa_dh,
                                   tm_c, False, ctx_out, "gqa_proj_ctx")
            ql, kl, vl = _qkv_proj(xl, mod, False, g_mix, gqa_tabs, w, gq, gk, GQA_HEADS, GQA_KV_HEADS, gqa_dh,
                                   tm_l, True, True, "gqa_proj")
            cfg = dict(groups=GQA_KV_HEADS, q_lanes=per_kv * LANES, k_lanes=LANES, heads=per_kv, k_slot=False,
                       pair_v=False, tq=256)
            yl = _attention(ql, kc, vc, kl, vl, name="gqa_attn", **cfg)
            yc = _attention(qc, kc, vc, None, None, name="gqa_attn_ctx", **cfg) if ctx_out else None
            w_mix = gqa_w_o[j].astype(BF16)

        w_in, w_out = ffn_w_in[i].astype(BF16), ffn_w_out[i].astype(BF16)
        xl = _post_ffn(xl, yl, mod, False, g_mix, g_ffn, d_skip, w_mix, w_in, w_out, glu, tm_l)
        if ctx_out:
            xc = _post_ffn(xc, yc, mod, True, g_mix, g_ffn, d_skip, w_mix, w_in, w_out, glu, tm_c)
    return xl
```

```python
import functools
import math

import jax
import jax.numpy as jnp
from jax import lax
from jax.experimental import pallas as pl
from jax.experimental.pallas import tpu as pltpu

F32 = jnp.float32
BF16 = jnp.bfloat16

EPS = 1e-6
ROPE_THETA = 10000.0
GRID_W = 64
NEG_INF = -1e30
LOG2E = math.log2(math.e)

MLA_HEADS = 16
MLA_NOPE = 64
MLA_ROPE = 32
MLA_QK = MLA_NOPE + MLA_ROPE
MLA_V = 64
MLA_Q_LORA = 384
MLA_KV_LORA = 256
MLA_SLOT = 128

S5_GROUP = 16
S5_STATE = 64
S5_CHUNK = 16

NA_HEADS = 16
NA_WIN_H = 8
NA_WIN_W = 16
NA_QROWS = 4
NA_BAND = NA_QROWS + NA_WIN_H

GQA_HEADS = 8
GQA_KV_HEADS = 2

LANES = 128
VMEM_LIMIT = 56 * 1024 * 1024


def _cparams(*sem):
    return pltpu.CompilerParams(dimension_semantics=sem, vmem_limit_bytes=VMEM_LIMIT)


def _modulate(x, g, shift, scale):
    ms = jnp.mean(x * x, axis=-1, keepdims=True)
    return x * lax.rsqrt(ms + EPS) * g * (1.0 + scale) + shift


def _rms_rows(x, n):
    ss = jnp.sum(x * x, axis=-1, keepdims=True)
    return lax.rsqrt(ss * (1.0 / n) + EPS)


def _silu(x):
    return x * jax.nn.sigmoid(x)


def _gelu_tanh(x):
    return 0.5 * x * (1.0 + jnp.tanh(math.sqrt(2.0 / math.pi) * (x + 0.044715 * x * x * x)))


def _rope_lanes(x, cos, sin_signed, lower, shift):
    width = x.shape[-1]
    partner = jnp.where(lower, pltpu.roll(x, width - shift, 1), pltpu.roll(x, shift, 1))
    return x * cos + partner * sin_signed


def _dot(a, b):
    return jnp.dot(a, b, preferred_element_type=F32)


def _dot_nt(a, b):
    return lax.dot_general(a, b, (((1,), (1,)), ((), ())), preferred_element_type=F32)


def _ada_kernel(c_ref, w_ref, b_ref, o_ref):
    c = c_ref[...]
    o_ref[...] = _dot(_silu(c), w_ref[...]) + b_ref[...]


def _ada_terms(cvec, ada_w, ada_b):
    depth, d, n6 = ada_w.shape
    rows = cvec.shape[0]
    tn = 1536
    return pl.pallas_call(
        _ada_kernel,
        out_shape=jax.ShapeDtypeStruct((depth, rows, n6), F32),
        grid=(depth, n6 // tn),
        in_specs=[
            pl.BlockSpec((rows, d), lambda i, j: (0, 0)),
            pl.BlockSpec((None, d, tn), lambda i, j: (i, 0, j)),
            pl.BlockSpec((None, 1, tn), lambda i, j: (i, 0, j)),
        ],
        out_specs=pl.BlockSpec((None, rows, tn), lambda i, j: (i, 0, j)),
        compiler_params=_cparams("parallel", "parallel"),
        name="ada_terms",
    )(cvec, ada_w, ada_b.reshape(depth, 1, n6))


class _Mod:
    def __init__(self, table, d, ctx_row):
        self.table, self.d, self.ctx_row = table, d, ctx_row

    def spec(self, term, is_ctx, batch_axis):
        ctx_row = self.ctx_row
        if is_ctx:
            return pl.BlockSpec((None, 1, self.d), lambda *ids: (ctx_row, 0, term))
        return pl.BlockSpec((None, 1, self.d), lambda *ids: (ids[batch_axis], 0, term))


def _const_spec(shape):
    nd = len(shape)
    return pl.BlockSpec(shape, lambda *ids: (0,) * nd)


def _tok_spec(tm, width):
    return pl.BlockSpec((None, tm, width), lambda b, j: (b, j, 0))


def _mla_proj_kernel(x_ref, sh_ref, sc_ref, g_ref, cos_ref, sin_ref, w_in_ref, gq_ref, gkv_ref,
                     w_uq_ref, w_ukv_ref, gqn_ref, gkn_ref, gkr_ref, gkrs_ref,
                     q_ref, k_ref, v_ref, *, use_rope, need_q):
    h = _modulate(x_ref[...], g_ref[...], sh_ref[...], sc_ref[...]).astype(BF16)
    z = _dot(h, w_in_ref[...])
    lane = lax.broadcasted_iota(jnp.int32, (1, MLA_SLOT), 1)
    lower = ((lane - MLA_NOPE) % (MLA_ROPE // 2)) < (MLA_ROPE // 4)
    if use_rope:
        cos, sin = cos_ref[...], sin_ref[...]

    kv0 = MLA_Q_LORA
    zkv = z[:, kv0:kv0 + MLA_KV_LORA]
    ckv = (zkv * _rms_rows(zkv, MLA_KV_LORA) * gkv_ref[...]).astype(BF16)
    kv = _dot(ckv, w_ukv_ref[...])
    v_ref[...] = kv[:, MLA_HEADS * MLA_SLOT:].astype(BF16)
    r0 = kv0 + MLA_KV_LORA
    zr = z[:, r0:r0 + MLA_SLOT]
    zrs = z[:, r0 + MLA_SLOT:r0 + 2 * MLA_SLOT]
    ss_rope = jnp.sum(zr * zr, axis=-1, keepdims=True)
    if use_rope:
        kr = zr * gkr_ref[...] * cos + zrs * gkrs_ref[...] * sin
    else:
        kr = zr * gkr_ref[...]
    gkn = gkn_ref[...]
    for hd in range(MLA_HEADS):
        seg = kv[:, hd * MLA_SLOT:(hd + 1) * MLA_SLOT]
        ss = jnp.sum(seg * seg, axis=-1, keepdims=True) + ss_rope
        r = lax.rsqrt(ss * (1.0 / MLA_QK) + EPS)
        k_ref[:, hd * MLA_SLOT:(hd + 1) * MLA_SLOT] = ((seg * gkn + kr) * r).astype(BF16)

    if need_q:
        zq = z[:, :MLA_Q_LORA]
        cq = (zq * _rms_rows(zq, MLA_Q_LORA) * gq_ref[...]).astype(BF16)
        qp = _dot(cq, w_uq_ref[...])
        gqn = gqn_ref[...] * (MLA_QK ** -0.5 * LOG2E)
        for hd in range(MLA_HEADS):
            seg = qp[:, hd * MLA_SLOT:(hd + 1) * MLA_SLOT]
            qn = seg * _rms_rows(seg, MLA_QK) * gqn
            if use_rope:
                qn = _rope_lanes(qn, cos, sin, lower, MLA_ROPE // 4)
            q_ref[:, hd * MLA_SLOT:(hd + 1) * MLA_SLOT] = qn.astype(BF16)
    else:
        q_ref[...] = jnp.zeros_like(q_ref)


def _mla_proj(x, mod, is_ctx, norm_g, tabs, wts, tm, need_q=True):
    b, l, d = x.shape
    use_rope = not is_ctx
    cos, sin = tabs
    qw = MLA_HEADS * MLA_SLOT
    tab_spec = pl.BlockSpec((tm, MLA_SLOT), lambda bb, j: (j, 0))
    in_specs = [_tok_spec(tm, d), mod.spec(0, is_ctx, 0), mod.spec(1, is_ctx, 0), _const_spec((1, d)),
                tab_spec, tab_spec] + [_const_spec(w.shape) for w in wts]
    out_q = qw if need_q else LANES
    return pl.pallas_call(
        functools.partial(_mla_proj_kernel, use_rope=use_rope, need_q=need_q),
        out_shape=(jax.ShapeDtypeStruct((b, l, out_q), BF16),
                   jax.ShapeDtypeStruct((b, l, qw), BF16),
                   jax.ShapeDtypeStruct((b, l, MLA_HEADS * MLA_V), BF16)),
        grid=(b, l // tm),
        in_specs=in_specs,
        out_specs=(_tok_spec(tm, out_q), _tok_spec(tm, qw), _tok_spec(tm, MLA_HEADS * MLA_V)),
        compiler_params=_cparams("parallel", "parallel"),
        name="mla_proj_ctx" if is_ctx else "mla_proj",
    )(x, mod.table, mod.table, norm_g, cos, sin, *wts)


def _qkv_proj_kernel(x_ref, sh_ref, sc_ref, g_ref, cos_ref, sin_ref, w_ref, gq_ref, gk_ref,
                     q_ref, k_ref, v_ref, *, n_q, n_kv, dh, use_rope, need_q, q_scale):
    h = _modulate(x_ref[...], g_ref[...], sh_ref[...], sc_ref[...]).astype(BF16)
    z = _dot(h, w_ref[...])
    q_cols, kv_cols = n_q * dh, n_kv * dh
    v_ref[...] = z[:, q_cols + kv_cols:].astype(BF16)
    lane = lax.broadcasted_iota(jnp.int32, (1, LANES), 1)
    if use_rope:
        cos, sin = cos_ref[...], sin_ref[...]
        lower = (lane % (dh // 2)) < (dh // 4)

    def normed(seg, gain):
        sq = seg * seg
        if dh == LANES:
            r = lax.rsqrt(jnp.sum(sq, axis=-1, keepdims=True) * (1.0 / dh) + EPS)
        else:
            first = lane < dh
            s_all = jnp.sum(sq, axis=-1, keepdims=True)
            s_first = jnp.sum(jnp.where(first, sq, 0.0), axis=-1, keepdims=True)
            r = lax.rsqrt(jnp.where(first, s_first, s_all - s_first) * (1.0 / dh) + EPS)
        out = seg * r * gain
        if use_rope:
            out = _rope_lanes(out, cos, sin, lower, dh // 4)
        return out.astype(BF16)

    gk = gk_ref[...]
    for s in range(kv_cols // LANES):
        k_ref[:, s * LANES:(s + 1) * LANES] = normed(z[:, q_cols + s * LANES:q_cols + (s + 1) * LANES], gk)
    if need_q:
        gq = gq_ref[...] * q_scale
        for s in range(q_cols // LANES):
            q_ref[:, s * LANES:(s + 1) * LANES] = normed(z[:, s * LANES:(s + 1) * LANES], gq)
    else:
        q_ref[...] = jnp.zeros_like(q_ref)


def _qkv_proj(x, mod, is_ctx, norm_g, tabs, w, gq, gk, n_q, n_kv, dh, tm, use_rope, need_q, name):
    b, l, d = x.shape
    cos, sin = tabs
    q_cols, kv_cols = n_q * dh, n_kv * dh
    if not need_q:
        w = w[:, q_cols:]
    tab_spec = pl.BlockSpec((tm, LANES), lambda bb, j: (j, 0))
    out_q = q_cols if need_q else LANES
    kern = functools.partial(_qkv_proj_kernel, n_q=n_q if need_q else 0, n_kv=n_kv, dh=dh,
                             use_rope=use_rope, need_q=need_q, q_scale=dh ** -0.5 * LOG2E)
    return pl.pallas_call(
        kern,
        out_shape=(jax.ShapeDtypeStruct((b, l, out_q), BF16),
                   jax.ShapeDtypeStruct((b, l, kv_cols), BF16),
                   jax.ShapeDtypeStruct((b, l, kv_cols), BF16)),
        grid=(b, l // tm),
        in_specs=[_tok_spec(tm, d), mod.spec(0, is_ctx, 0), mod.spec(1, is_ctx, 0), _const_spec((1, d)),
                  tab_spec, tab_spec, _const_spec(w.shape), _const_spec(gq.shape), _const_spec(gk.shape)],
        out_specs=(_tok_spec(tm, out_q), _tok_spec(tm, kv_cols), _tok_spec(tm, kv_cols)),
        compiler_params=_cparams("parallel", "parallel"),
        name=name,
    )(x, mod.table, mod.table, norm_g, cos, sin, w, gq, gk)


def _softmax_pv(s_parts, v_parts):
    m = s_parts[0].max(axis=-1, keepdims=True)
    for s in s_parts[1:]:
        m = jnp.maximum(m, s.max(axis=-1, keepdims=True))
    denom = None
    acc = None
    for s, v in zip(s_parts, v_parts):
        p = jnp.exp2(s - m)
        d = jnp.sum(p, axis=-1, keepdims=True)
        o = _dot(p.astype(BF16), v)
        denom = d if denom is None else denom + d
        acc = o if acc is None else acc + o
    return acc * (1.0 / denom)


def _split_pair(q2, first):
    zero = jnp.zeros_like(q2)
    return jnp.concatenate([jnp.where(first, q2, zero), jnp.where(first, zero, q2)], axis=0)


def _attn_kernel(*refs, tq, heads, mode, has_latent):
    if has_latent:
        q_ref, kc_ref, vc_ref, kl_ref, vl_ref, o_ref = refs
    else:
        q_ref, kc_ref, vc_ref, o_ref = refs
        kl_ref = vl_ref = None
    n_tiles = q_ref.shape[0] // tq
    lane = lax.broadcasted_iota(jnp.int32, (1, LANES), 1)
    first = lane < (LANES // 2)

    def attend(q, ksl):
        s_parts = [_dot_nt(q, kc_ref[:, ksl])]
        v_parts = [vc_ref[...]]
        if has_latent:
            s_parts.append(_dot_nt(q, kl_ref[:, ksl]))
            v_parts.append(vl_ref[...])
        return _softmax_pv(s_parts, v_parts)

    def tile(i, carry):
        rows = pl.ds(pl.multiple_of(i * tq, tq), tq)
        if mode == "slots":
            outs = [attend(q_ref[rows, e * LANES:(e + 1) * LANES], slice(e * LANES, (e + 1) * LANES))
                    for e in range(2)]
            o_ref[rows, :] = jnp.where(first, outs[0], outs[1]).astype(o_ref.dtype)
        elif mode == "shared":
            q = jnp.concatenate([q_ref[rows, e * LANES:(e + 1) * LANES] for e in range(heads)], axis=0)
            o = attend(q, slice(0, LANES))
            for e in range(heads):
                o_ref[rows, e * LANES:(e + 1) * LANES] = o[e * tq:(e + 1) * tq].astype(o_ref.dtype)
        else:
            o = attend(_split_pair(q_ref[rows, :], first), slice(0, LANES))
            o_ref[rows, :] = jnp.where(first, o[:tq], o[tq:]).astype(o_ref.dtype)
        return carry

    lax.fori_loop(0, n_tiles, tile, 0)


def _attention(q, kc, vc, kl, vl, *, groups, q_lanes, k_lanes, heads, mode, tq, name):
    b, l, _ = q.shape
    c = kc.shape[1]
    has_latent = kl is not None
    tq = min(tq, l)
    o_lanes = heads * LANES if mode == "shared" else LANES

    def gspec(n, w):
        return pl.BlockSpec((None, n, w), lambda bb, g: (bb, 0, g))

    in_specs = [gspec(l, q_lanes), gspec(c, k_lanes), gspec(c, LANES)]
    args = [q, kc, vc]
    if has_latent:
        s = kl.shape[1]
        in_specs += [gspec(s, k_lanes), gspec(s, LANES)]
        args += [kl, vl]
    return pl.pallas_call(
        functools.partial(_attn_kernel, tq=tq, heads=heads, mode=mode, has_latent=has_latent),
        out_shape=jax.ShapeDtypeStruct((b, l, groups * o_lanes), BF16),
        grid=(b, groups),
        in_specs=in_specs,
        out_specs=gspec(l, o_lanes),
        compiler_params=_cparams("parallel", "parallel"),
        name=name,
    )(*args)


def _na_kernel(q_ref, kc_ref, vc_ref, kl_ref, vl_ref, bias_ref, o_ref, *, n_blocks):
    tq = NA_QROWS * GRID_W
    band = NA_BAND * GRID_W
    rows_total = n_blocks * NA_QROWS
    lane = lax.broadcasted_iota(jnp.int32, (1, LANES), 1)
    first = lane < (LANES // 2)

    def block(i, carry):
        rows = pl.ds(pl.multiple_of(i * tq, tq), tq)
        band_row0 = jnp.clip(i * NA_QROWS - NA_WIN_H // 2, 0, rows_total - NA_BAND)
        keys = pl.ds(pl.multiple_of(band_row0 * GRID_W, GRID_W), band)
        variant = jnp.where(i == 0, 0, jnp.where(i == n_blocks - 1, 2, 1))
        q = _split_pair(q_ref[rows, :], first)
        bias = jnp.concatenate([bias_ref[0, variant], bias_ref[1, variant]], axis=0)
        s_ctx = _dot_nt(q, kc_ref[...])
        s_lat = _dot_nt(q, kl_ref[keys, :]) + bias
        o = _softmax_pv([s_ctx, s_lat], [vc_ref[...], vl_ref[keys, :]])
        o_ref[rows, :] = jnp.where(first, o[:tq], o[tq:]).astype(o_ref.dtype)
        return carry

    lax.fori_loop(0, n_blocks, block, 0)


def _na_attention(q, kc, vc, kl, vl, bias):
    b, s, w = q.shape
    c = kc.shape[1]
    pairs = w // LANES
    n_blocks = s // (NA_QROWS * GRID_W)

    def gspec(n):
        return pl.BlockSpec((None, n, LANES), lambda g, bb: (bb, 0, g))

    return pl.pallas_call(
        functools.partial(_na_kernel, n_blocks=n_blocks),
        out_shape=jax.ShapeDtypeStruct((b, s, w), BF16),
        grid=(pairs, b),
        in_specs=[gspec(s), gspec(c), gspec(c), gspec(s), gspec(s),
                  pl.BlockSpec((2,) + bias.shape[1:], lambda g, bb: (g, 0, 0, 0))],
        out_specs=gspec(s),
        compiler_params=_cparams("parallel", "parallel"),
        name="na_attention",
    )(q, kc, vc, kl, vl, bias)


def _na_bias_table(rpb, rows):
    n_blocks = rows // NA_QROWS
    tables = []
    for i in (0, 1, n_blocks - 1):
        band0 = min(max(i * NA_QROWS - NA_WIN_H // 2, 0), rows - NA_BAND)
        qr = i * NA_QROWS + jnp.arange(NA_QROWS)
        kr = band0 + jnp.arange(NA_BAND)
        r0 = jnp.clip(qr - NA_WIN_H // 2, 0, rows - NA_WIN_H)
        row_ok = (kr[None, :] >= r0[:, None]) & (kr[None, :] < r0[:, None] + NA_WIN_H)
        row_idx = jnp.clip(kr[None, :] - qr[:, None] + NA_WIN_H - 1, 0, 2 * NA_WIN_H - 2)
        j = jnp.arange(GRID_W)
        c0 = jnp.clip(j - NA_WIN_W // 2, 0, GRID_W - NA_WIN_W)
        col_ok = (j[None, :] >= c0[:, None]) & (j[None, :] < c0[:, None] + NA_WIN_W)
        col_idx = jnp.clip(j[None, :] - j[:, None] + NA_WIN_W - 1, 0, 2 * NA_WIN_W - 2)
        bias = rpb[:, row_idx][:, :, :, col_idx]
        ok = row_ok[:, :, None, None] & col_ok[None, None, :, :]
        bias = jnp.where(ok[None], bias.astype(F32) * LOG2E, NEG_INF)
        tables.append(bias.transpose(0, 1, 3, 2, 4).reshape(rpb.shape[0], NA_QROWS * GRID_W, NA_BAND * GRID_W))
    return jnp.stack(tables, axis=1)


def _s5_kernel(uc_ref, ul_ref, kc_ref, ws_ref, wo_ref, dec_ref, yc_ref, yl_ref, s_ref, h_ref, *, batch):
    half = 2 * S5_STATE
    lane = lax.broadcasted_iota(jnp.int32, (1, half), 1)
    fwd = lane < S5_STATE
    dec = dec_ref[...]
    d_re, d_im = dec[:, :half], dec[:, half:]

    def scan(u_ref, y_ref, n_chunks, init_re, init_im):
        rows = n_chunks * batch
        u = u_ref[...]
        s_ref[0:rows, :] = _dot(u, ws_ref[...])

        def step(i, carry):
            st_re, st_im = carry
            rf = pl.ds(pl.multiple_of(i * batch, batch), batch)
            rb = pl.ds(pl.multiple_of((n_chunks - 1 - i) * batch, batch), batch)
            h_ref[rf, 0:S5_STATE] = st_re[:, :S5_STATE]
            h_ref[rf, half:half + S5_STATE] = st_im[:, :S5_STATE]
            h_ref[rb, S5_STATE:half] = st_re[:, S5_STATE:]
            h_ref[rb, half + S5_STATE:] = st_im[:, S5_STATE:]
            in_re = jnp.where(fwd, s_ref[rf, 0:half], s_ref[rb, 0:half])
            in_im = jnp.where(fwd, s_ref[rf, half:], s_ref[rb, half:])
            return (d_re * st_re - d_im * st_im + in_re, d_re * st_im + d_im * st_re + in_im)

        st_re, st_im = lax.fori_loop(0, n_chunks, step, (init_re, init_im))
        y = _dot(u, kc_ref[...]) + _dot(h_ref[0:rows, :].astype(BF16), wo_ref[...])
        y_ref[...] = y.astype(y_ref.dtype)
        return st_re, st_im

    zero = jnp.zeros((batch, half), F32)
    ctx_re, ctx_im = scan(uc_ref, yc_ref, uc_ref.shape[0] // batch, zero, zero)
    scan(ul_ref, yl_ref, ul_ref.shape[0] // batch, ctx_re, ctx_im)


def _s5_scan(uc, ul, kc, ws, wo, dec, batch):
    groups, rows_c, width = uc.shape
    rows_l = ul.shape[1]

    def gspec(n, w):
        return pl.BlockSpec((None, n, w), lambda g: (g, 0, 0))

    return pl.pallas_call(
        functools.partial(_s5_kernel, batch=batch),
        out_shape=(jax.ShapeDtypeStruct((groups, rows_c, width), BF16),
                   jax.ShapeDtypeStruct((groups, rows_l, width), BF16)),
        grid=(groups,),
        in_specs=[gspec(rows_c, width), gspec(rows_l, width), gspec(width, width), gspec(width, width),
                  gspec(width, width), gspec(1, width)],
        out_specs=(gspec(rows_c, width), gspec(rows_l, width)),
        scratch_shapes=[pltpu.VMEM((rows_l, width), F32), pltpu.VMEM((rows_l, width), F32)],
        compiler_params=_cparams("parallel"),
        name="s5_scan",
    )(uc, ul, kc, ws, wo, dec)


def _s5_matrices(a_re, a_im, log_dt, b_re, b_im, c_re, c_im):
    t = S5_CHUNK
    dt = jnp.exp(log_dt.astype(F32))[..., None]
    a_re, a_im = a_re.astype(F32), a_im.astype(F32)
    lam_re, lam_im = dt * a_re, dt * a_im

    def power(j):
        jj = jnp.asarray(j, F32)
        mag = jnp.exp(lam_re[..., None] * jj)
        return mag * jnp.cos(lam_im[..., None] * jj), mag * jnp.sin(lam_im[..., None] * jj)

    ab_re, ab_im = power(jnp.ones((1,)))
    ab_re, ab_im = ab_re[..., 0], ab_im[..., 0]
    den = a_re * a_re + a_im * a_im
    nr = ab_re - 1.0
    f_re = (nr * a_re + ab_im * a_im) / den
    f_im = (ab_im * a_re - nr * a_im) / den
    bb_re = f_re[..., None] * b_re - f_im[..., None] * b_im
    bb_im = f_re[..., None] * b_im + f_im[..., None] * b_re
    c_re, c_im = c_re.astype(F32), c_im.astype(F32)

    hi = lax.Precision.HIGHEST
    lags = jnp.arange(t)
    pw_re, pw_im = power(lags)
    cr, ci = c_re[:, :, :, None, :], c_im[:, :, :, None, :]
    pr, pi = pw_re.transpose(0, 1, 3, 2)[:, :, None], pw_im.transpose(0, 1, 3, 2)[:, :, None]
    cb_re = cr * pr - ci * pi
    cb_im = cr * pi + ci * pr
    conv = (jnp.einsum('dgclp,dgpi->dglic', cb_re, bb_re, precision=hi)
            - jnp.einsum('dgclp,dgpi->dglic', cb_im, bb_im, precision=hi))
    src, dst = lags[:, None], lags[None, :]
    lag_f = dst - src
    k_f = jnp.where((lag_f >= 0)[None, :, None, :, None], conv[0][:, jnp.clip(lag_f, 0, t - 1)].transpose(0, 1, 3, 2, 4), 0.0)
    lag_b = src - dst
    k_b = jnp.where((lag_b >= 0)[None, :, None, :, None], conv[1][:, jnp.clip(lag_b, 0, t - 1)].transpose(0, 1, 3, 2, 4), 0.0)
    g = a_re.shape[1]
    cg = b_re.shape[-1]
    k_mat = (k_f + k_b).reshape(g, t * cg, t * cg)

    def state_w(d, expo):
        p_re, p_im = power(expo)
        w_re = p_re[d][..., None] * bb_re[d][:, :, None, :] - p_im[d][..., None] * bb_im[d][:, :, None, :]
        w_im = p_re[d][..., None] * bb_im[d][:, :, None, :] + p_im[d][..., None] * bb_re[d][:, :, None, :]
        to_rows = lambda w: w.transpose(0, 2, 3, 1).reshape(g, t * cg, -1)
        return to_rows(w_re), to_rows(w_im)

    wf_re, wf_im = state_w(0, t - 1 - lags)
    wb_re, wb_im = state_w(1, lags)
    w_state = jnp.concatenate([wf_re, wb_re, wf_im, wb_im], axis=-1)

    def out_w(d, expo):
        p_re, p_im = power(expo)
        m_re = c_re[d][:, :, :, None] * p_re[d][:, None, :, :] - c_im[d][:, :, :, None] * p_im[d][:, None, :, :]
        m_im = c_re[d][:, :, :, None] * p_im[d][:, None, :, :] + c_im[d][:, :, :, None] * p_re[d][:, None, :, :]
        to_cols = lambda w: w.transpose(0, 2, 3, 1).reshape(g, -1, t * cg)
        return to_cols(m_re), -to_cols(m_im)

    of_re, of_im = out_w(0, lags + 1)
    ob_re, ob_im = out_w(1, t - lags)
    w_out = jnp.concatenate([of_re, ob_re, of_im, ob_im], axis=1)

    dT_re, dT_im = power(jnp.full((1,), float(t)))
    dT_re, dT_im = dT_re[..., 0], dT_im[..., 0]
    decay = jnp.concatenate([dT_re[0], dT_re[1], dT_im[0], dT_im[1]], axis=-1)[:, None, :]
    return k_mat.astype(BF16), w_state.astype(BF16), w_out.astype(BF16), decay


def _s5_proj_kernel(x_ref, sh_ref, sc_ref, g_ref, u_ref):
    u_ref[...] = _modulate(x_ref[...], g_ref[...], sh_ref[...], sc_ref[...]).astype(BF16)


def _s5_proj(x, mod, is_ctx, norm_g, tm):
    b, l, d = x.shape
    return pl.pallas_call(
        _s5_proj_kernel,
        out_shape=jax.ShapeDtypeStruct((b, l, d), BF16),
        grid=(b, l // tm),
        in_specs=[_tok_spec(tm, d), mod.spec(0, is_ctx, 0), mod.spec(1, is_ctx, 0), _const_spec((1, d))],
        out_specs=_tok_spec(tm, d),
        compiler_params=_cparams("parallel", "parallel"),
        name="s5_proj",
    )(x, mod.table, mod.table, norm_g)


def _to_chunks(u):
    b, l, d = u.shape
    g = d // S5_GROUP
    u = u.reshape(b, l // S5_CHUNK, S5_CHUNK, g, S5_GROUP).transpose(3, 1, 0, 2, 4)
    return u.reshape(g, (l // S5_CHUNK) * b, S5_CHUNK * S5_GROUP)


def _from_chunks(y, b):
    g, rows, _ = y.shape
    n = rows // b
    y = y.reshape(g, n, b, S5_CHUNK, S5_GROUP).transpose(2, 1, 3, 0, 4)
    return y.reshape(b, n * S5_CHUNK, g * S5_GROUP)


def _post_ffn_kernel(x_ref, y_ref, m_ref, gmix_ref, gffn_ref, dskip_ref, w_mix_ref, w_in_ref, w_out_ref,
                     o_ref, acc_ref, *, glu, hidden, chunk):
    sh, sc, gt, sh2, sc2, gt2 = (m_ref[i:i + 1, :] for i in range(6))
    x = x_ref[...]
    if glu:
        yv = dskip_ref[...] * _modulate(x, gmix_ref[...], sh, sc) + y_ref[...].astype(F32)
        ab = _dot(_gelu_tanh(yv).astype(BF16), w_mix_ref[...])
        n = ab.shape[-1] // 2
        mix = ab[:, :n] * jax.nn.sigmoid(ab[:, n:])
    else:
        mix = _dot(y_ref[...], w_mix_ref[...])
    x = x + gt * mix
    h = _modulate(x, gffn_ref[...], sh2, sc2).astype(BF16)
    for c0 in range(0, hidden, chunk):
        a = _dot(h, w_in_ref[:, c0:c0 + chunk])
        b = _dot(h, w_in_ref[:, hidden + c0:hidden + c0 + chunk])
        part = _dot((_silu(a) * b).astype(BF16), w_out_ref[c0:c0 + chunk, :])
        if c0 == 0:
            acc_ref[...] = part
        else:
            acc_ref[...] += part
    o_ref[...] = x + gt2 * acc_ref[...]


def _post_ffn(x, y, mod, is_ctx, g_mix, g_ffn, d_skip, w_mix, w_in, w_out, glu, tm):
    b, l, d = x.shape
    hidden = w_out.shape[0]
    ctx_row = mod.ctx_row
    if is_ctx:
        mspec = pl.BlockSpec((None, 6, d), lambda bb, j: (ctx_row, 0, 0))
    else:
        mspec = pl.BlockSpec((None, 6, d), lambda bb, j: (bb, 0, 0))
    single = dict(pipeline_mode=pl.Buffered(1))

    def wspec(shape):
        nd = len(shape)
        return pl.BlockSpec(shape, lambda *ids: (0,) * nd, **single)

    return pl.pallas_call(
        functools.partial(_post_ffn_kernel, glu=glu, hidden=hidden, chunk=hidden // 2),
        out_shape=jax.ShapeDtypeStruct((b, l, d), F32),
        grid=(b, l // tm),
        in_specs=[_tok_spec(tm, d), _tok_spec(tm, y.shape[-1]), mspec, _const_spec((1, d)), _const_spec((1, d)),
                  _const_spec((1, d)), wspec(w_mix.shape), wspec(w_in.shape), wspec(w_out.shape)],
        out_specs=_tok_spec(tm, d),
        scratch_shapes=[pltpu.VMEM((tm, d), F32)],
        compiler_params=_cparams("parallel", "parallel"),
        name="post_ffn_ctx" if is_ctx else "post_ffn",
    )(x, y, mod.table.reshape(mod.table.shape[0], 6, d), g_mix, g_ffn, d_skip, w_mix, w_in, w_out)


def _rope_tables(n_tokens, rot_dim, offset, width):
    t = jnp.arange(n_tokens, dtype=jnp.int32)
    axis_dim = rot_dim // 2
    inv_freq = ROPE_THETA ** (-jnp.arange(0, axis_dim, 2, dtype=F32) / axis_dim)
    lane = jnp.arange(width)
    j = lane - offset
    in_rot = (j >= 0) & (j < rot_dim)
    jc = jnp.clip(j, 0, rot_dim - 1)
    use_col = jc >= axis_dim
    within = jc % axis_dim
    freq = inv_freq[within % (axis_dim // 2)]
    lower = within < (axis_dim // 2)
    pos = jnp.where(use_col[None, :], (t % GRID_W)[:, None], (t // GRID_W)[:, None]).astype(F32)
    ang = pos * freq[None, :]
    cos = jnp.where(in_rot[None, :], jnp.cos(ang), 1.0)
    sin = jnp.where(in_rot[None, :], jnp.where(lower[None, :], -jnp.sin(ang), jnp.sin(ang)), 0.0)
    return cos, sin


def _mla_weights(w_in, g_q, g_kv, w_uq, w_ukv, g_qn, g_kn):
    d = w_in.shape[0]
    nh, slot = MLA_HEADS, MLA_SLOT
    rope0 = MLA_Q_LORA + MLA_KV_LORA
    w_rope = w_in[:, rope0:]
    quarter = MLA_ROPE // 4
    partner = jnp.arange(MLA_ROPE).reshape(-1, 2, quarter)[:, ::-1, :].reshape(-1)

    def slotted(cols):
        return jnp.pad(cols, ((0, 0), (MLA_NOPE, slot - MLA_QK)))

    w_in_p = jnp.concatenate([w_in[:, :rope0], slotted(w_rope), slotted(w_rope[:, partner])], axis=1)
    w_uq_p = jnp.pad(w_uq.reshape(-1, nh, MLA_QK), ((0, 0), (0, 0), (0, slot - MLA_QK))).reshape(-1, nh * slot)
    kv = w_ukv.reshape(-1, nh, MLA_NOPE + MLA_V)
    w_k = jnp.pad(kv[:, :, :MLA_NOPE], ((0, 0), (0, 0), (0, slot - MLA_NOPE))).reshape(-1, nh * slot)
    w_v = kv[:, :, MLA_NOPE:].reshape(-1, nh * MLA_V)
    w_ukv_p = jnp.concatenate([w_k, w_v], axis=1)
    pad_slot = lambda g: jnp.pad(g, (0, slot - g.shape[0]))[None, :]
    gqn = pad_slot(g_qn)
    gkn_nope = pad_slot(g_kn[:MLA_NOPE])
    gkr = jnp.pad(g_kn[MLA_NOPE:], (MLA_NOPE, slot - MLA_QK))[None, :]
    gkrs = jnp.pad(g_kn[MLA_NOPE:][partner], (MLA_NOPE, slot - MLA_QK))[None, :]
    return (w_in_p.astype(BF16), g_q[None, :], g_kv[None, :], w_uq_p.astype(BF16), w_ukv_p.astype(BF16),
            gqn, gkn_nope, gkr, gkrs)


def _tile_gain(g):
    return jnp.tile(g, LANES // g.shape[0])[None, :]


def kernel(x, c, ctx, c_ctx, ada_w, ada_b, norm_mix, norm_ffn, ffn_w_in, ffn_w_out,
           mla_w_in, mla_g_q, mla_g_kv, mla_w_uq, mla_w_ukv, mla_g_qn, mla_g_kn, mla_w_o,
           s5_a_re, s5_a_im, s5_log_dt, s5_b_re, s5_b_im, s5_c_re, s5_c_im, s5_d, s5_w_glu,
           na_w_qkv, na_g_qn, na_g_kn, na_rpb, na_w_o,
           gqa_w_qkv, gqa_g_qn, gqa_g_kn, gqa_w_o):
    batch, seq, d = x.shape
    n_ctx = ctx.shape[1]
    depth = ada_w.shape[0]
    n_mixers = 4
    ctx_row = batch
    mod_rows = -(-(batch + 1) // 8) * 8
    cvec = jnp.concatenate([c, c_ctx[None, :], jnp.zeros((mod_rows - batch - 1, d), F32)], axis=0)
    ada = _ada_terms(cvec, ada_w, ada_b).reshape(depth, mod_rows, 1, 6 * d)

    mla_tabs = _rope_tables(seq, MLA_ROPE, MLA_NOPE, MLA_SLOT)
    gqa_dh = d // GQA_HEADS
    gqa_tabs = _rope_tables(seq, gqa_dh, 0, LANES)
    ident_tabs = (jnp.ones((n_ctx, LANES), F32), jnp.zeros((n_ctx, LANES), F32))
    ident_tabs_l = (jnp.ones((seq, LANES), F32), jnp.zeros((seq, LANES), F32))
    zero_skip = jnp.zeros((1, d), F32)
    tm_l, tm_c = 512, 256

    xl, xc = x, ctx
    for i in range(depth):
        kind, j = i % n_mixers, i // n_mixers
        ctx_out = i < depth - 1
        mod = _Mod(ada[i], d, ctx_row)
        g_mix, g_ffn = norm_mix[i][None, :], norm_ffn[i][None, :]
        glu = False
        d_skip = zero_skip
        if kind == 0:
            wts = _mla_weights(mla_w_in[j], mla_g_q[j], mla_g_kv[j], mla_w_uq[j], mla_w_ukv[j],
                               mla_g_qn[j], mla_g_kn[j])
            qc, kc, vc = _mla_proj(xc, mod, True, g_mix, ident_tabs, wts, tm_c, need_q=ctx_out)
            ql, kl, vl = _mla_proj(xl, mod, False, g_mix, mla_tabs, wts, tm_c)
            cfg = dict(groups=MLA_HEADS // 2, q_lanes=2 * MLA_SLOT, k_lanes=2 * MLA_SLOT, heads=2,
                       mode="slots", tq=512)
            yl = _attention(ql, kc, vc, kl, vl, name="mla_attn", **cfg)
            yc = _attention(qc, kc, vc, None, None, name="mla_attn_ctx", **cfg) if ctx_out else None
            w_mix = mla_w_o[j].astype(BF16)
        elif kind == 1:
            uc = _to_chunks(_s5_proj(xc, mod, True, g_mix, tm_c))
            ul = _to_chunks(_s5_proj(xl, mod, False, g_mix, tm_l))
            mats = _s5_matrices(s5_a_re[j], s5_a_im[j], s5_log_dt[j], s5_b_re[j], s5_b_im[j],
                                s5_c_re[j], s5_c_im[j])
            yc, yl = _s5_scan(uc, ul, *mats, batch)
            yc, yl = _from_chunks(yc, batch), _from_chunks(yl, batch)
            w_mix = s5_w_glu[j].astype(BF16)
            glu = True
            d_skip = s5_d[j][None, :]
        elif kind == 2:
            dh = d // NA_HEADS
            w = na_w_qkv[j].astype(BF16)
            gq, gk = _tile_gain(na_g_qn[j]), _tile_gain(na_g_kn[j])
            qc, kc, vc = _qkv_proj(xc, mod, True, g_mix, ident_tabs, w, gq, gk, NA_HEADS, NA_HEADS, dh, tm_c,
                                   False, ctx_out, "na_proj_ctx")
            ql, kl, vl = _qkv_proj(xl, mod, False, g_mix, ident_tabs_l, w, gq, gk, NA_HEADS, NA_HEADS, dh, tm_l,
                                   False, True, "na_proj")
            bias = _na_bias_table(na_rpb[j], seq // GRID_W)
            yl = _na_attention(ql, kc, vc, kl, vl, bias)
            if ctx_out:
                yc = _attention(qc, kc, vc, None, None, groups=NA_HEADS // 2, q_lanes=LANES, k_lanes=LANES,
                                heads=2, mode="pair", tq=256, name="na_attn_ctx")
            w_mix = na_w_o[j].astype(BF16)
        else:
            w = gqa_w_qkv[j].astype(BF16)
            gq, gk = gqa_g_qn[j][None, :], gqa_g_kn[j][None, :]
            per_kv = GQA_HEADS // GQA_KV_HEADS
            qc, kc, vc = _qkv_proj(xc, mod, True, g_mix, ident_tabs, w, gq, gk, GQA_HEADS, GQA_KV_HEADS, gqa_dh,
                                   tm_c, False, ctx_out, "gqa_proj_ctx")
            ql, kl, vl = _qkv_proj(xl, mod, False, g_mix, gqa_tabs, w, gq, gk, GQA_HEADS, GQA_KV_HEADS, gqa_dh,
                                   tm_l, True, True, "gqa_proj")
            cfg = dict(groups=GQA_KV_HEADS, q_lanes=per_kv * LANES, k_lanes=LANES, heads=per_kv, mode="shared",
                       tq=128)
            yl = _attention(ql, kc, vc, kl, vl, name="gqa_attn", **cfg)
            yc = _attention(qc, kc, vc, None, None, name="gqa_attn_ctx", **cfg) if ctx_out else None
            w_mix = gqa_w_o[j].astype(BF16)

        w_in, w_out = ffn_w_in[i].astype(BF16), ffn_w_out[i].astype(BF16)
        xl = _post_ffn(xl, yl, mod, False, g_mix, g_ffn, d_skip, w_mix, w_in, w_out, glu, tm_l)
        if ctx_out:
            xc = _post_ffn(xc, yc, mod, True, g_mix, g_ffn, d_skip, w_mix, w_in, w_out, glu, tm_c)
    return xl
```

```python
import functools
import math

import jax
import jax.numpy as jnp
from jax import lax
from jax.experimental import pallas as pl
from jax.experimental.pallas import tpu as pltpu

F32 = jnp.float32
BF16 = jnp.bfloat16

EPS = 1e-6
ROPE_THETA = 10000.0
GRID_W = 64
NEG_INF = -1e30
LOG2E = math.log2(math.e)

MLA_HEADS = 16
MLA_NOPE = 64
MLA_ROPE = 32
MLA_QK = MLA_NOPE + MLA_ROPE
MLA_V = 64
MLA_Q_LORA = 384
MLA_KV_LORA = 256
MLA_SLOT = 128

S5_GROUP = 16
S5_STATE = 64
S5_CHUNK = 16

NA_HEADS = 16
NA_WIN_H = 8
NA_WIN_W = 16
NA_QROWS = 4
NA_BAND = NA_QROWS + NA_WIN_H

GQA_HEADS = 8
GQA_KV_HEADS = 2

LANES = 128
VMEM_LIMIT = 56 * 1024 * 1024


def _cparams(*sem):
    return pltpu.CompilerParams(dimension_semantics=sem, vmem_limit_bytes=VMEM_LIMIT)


def _modulate(x, g, shift, scale):
    ms = jnp.mean(x * x, axis=-1, keepdims=True)
    return x * lax.rsqrt(ms + EPS) * g * (1.0 + scale) + shift


def _rms_rows(x, n):
    ss = jnp.sum(x * x, axis=-1, keepdims=True)
    return lax.rsqrt(ss * (1.0 / n) + EPS)


def _silu(x):
    return x * jax.nn.sigmoid(x)


def _gelu_tanh(x):
    return 0.5 * x * (1.0 + jnp.tanh(math.sqrt(2.0 / math.pi) * (x + 0.044715 * x * x * x)))


def _rope_lanes(x, cos, sin_signed, lower, shift):
    width = x.shape[-1]
    partner = jnp.where(lower, pltpu.roll(x, width - shift, 1), pltpu.roll(x, shift, 1))
    return x * cos + partner * sin_signed


def _dot(a, b):
    return jnp.dot(a, b, preferred_element_type=F32)


def _dot_nt(a, b):
    return lax.dot_general(a, b, (((1,), (1,)), ((), ())), preferred_element_type=F32)


def _ada_kernel(c_ref, w_ref, b_ref, o_ref):
    c = c_ref[...]
    o_ref[...] = _dot(_silu(c), w_ref[...]) + b_ref[...]


def _ada_terms(cvec, ada_w, ada_b):
    depth, d, n6 = ada_w.shape
    rows = cvec.shape[0]
    tn = 1536
    return pl.pallas_call(
        _ada_kernel,
        out_shape=jax.ShapeDtypeStruct((depth, rows, n6), F32),
        grid=(depth, n6 // tn),
        in_specs=[
            pl.BlockSpec((rows, d), lambda i, j: (0, 0)),
            pl.BlockSpec((None, d, tn), lambda i, j: (i, 0, j)),
            pl.BlockSpec((None, 1, tn), lambda i, j: (i, 0, j)),
        ],
        out_specs=pl.BlockSpec((None, rows, tn), lambda i, j: (i, 0, j)),
        compiler_params=_cparams("parallel", "parallel"),
        name="ada_terms",
    )(cvec, ada_w, ada_b.reshape(depth, 1, n6))


class _Mod:
    def __init__(self, table, d, ctx_row):
        self.table, self.d, self.ctx_row = table, d, ctx_row

    def spec(self, term, is_ctx, batch_axis):
        ctx_row = self.ctx_row
        if is_ctx:
            return pl.BlockSpec((None, 1, self.d), lambda *ids: (ctx_row, 0, term))
        return pl.BlockSpec((None, 1, self.d), lambda *ids: (ids[batch_axis], 0, term))


def _const_spec(shape):
    nd = len(shape)
    return pl.BlockSpec(shape, lambda *ids: (0,) * nd)


def _tok_spec(tm, width):
    return pl.BlockSpec((None, tm, width), lambda b, j: (b, j, 0))


def _mla_proj_kernel(x_ref, sh_ref, sc_ref, g_ref, cos_ref, sin_ref, w_in_ref, gq_ref, gkv_ref,
                     w_uq_ref, w_ukv_ref, gqn_ref, gkn_ref, gkr_ref, gkrs_ref,
                     q_ref, k_ref, v_ref, *, use_rope, need_q):
    h = _modulate(x_ref[...], g_ref[...], sh_ref[...], sc_ref[...]).astype(BF16)
    z = _dot(h, w_in_ref[...])
    lane = lax.broadcasted_iota(jnp.int32, (1, MLA_SLOT), 1)
    lower = ((lane - MLA_NOPE) % (MLA_ROPE // 2)) < (MLA_ROPE // 4)
    if use_rope:
        cos, sin = cos_ref[...], sin_ref[...]

    kv0 = MLA_Q_LORA
    zkv = z[:, kv0:kv0 + MLA_KV_LORA]
    ckv = (zkv * _rms_rows(zkv, MLA_KV_LORA) * gkv_ref[...]).astype(BF16)
    kv = _dot(ckv, w_ukv_ref[...])
    v_ref[...] = kv[:, MLA_HEADS * MLA_SLOT:].astype(BF16)
    r0 = kv0 + MLA_KV_LORA
    zr = z[:, r0:r0 + MLA_SLOT]
    zrs = z[:, r0 + MLA_SLOT:r0 + 2 * MLA_SLOT]
    ss_rope = jnp.sum(zr * zr, axis=-1, keepdims=True)
    if use_rope:
        kr = zr * gkr_ref[...] * cos + zrs * gkrs_ref[...] * sin
    else:
        kr = zr * gkr_ref[...]
    gkn = gkn_ref[...]
    for hd in range(MLA_HEADS):
        seg = kv[:, hd * MLA_SLOT:(hd + 1) * MLA_SLOT]
        ss = jnp.sum(seg * seg, axis=-1, keepdims=True) + ss_rope
        r = lax.rsqrt(ss * (1.0 / MLA_QK) + EPS)
        k_ref[:, hd * MLA_SLOT:(hd + 1) * MLA_SLOT] = ((seg * gkn + kr) * r).astype(BF16)

    if need_q:
        zq = z[:, :MLA_Q_LORA]
        cq = (zq * _rms_rows(zq, MLA_Q_LORA) * gq_ref[...]).astype(BF16)
        qp = _dot(cq, w_uq_ref[...])
        gqn = gqn_ref[...] * (MLA_QK ** -0.5 * LOG2E)
        for hd in range(MLA_HEADS):
            seg = qp[:, hd * MLA_SLOT:(hd + 1) * MLA_SLOT]
            qn = seg * _rms_rows(seg, MLA_QK) * gqn
            if use_rope:
                qn = _rope_lanes(qn, cos, sin, lower, MLA_ROPE // 4)
            q_ref[:, hd * MLA_SLOT:(hd + 1) * MLA_SLOT] = qn.astype(BF16)
    else:
        q_ref[...] = jnp.zeros_like(q_ref)


def _mla_proj(x, mod, is_ctx, norm_g, tabs, wts, tm, need_q=True):
    b, l, d = x.shape
    use_rope = not is_ctx
    cos, sin = tabs
    qw = MLA_HEADS * MLA_SLOT
    tab_spec = pl.BlockSpec((tm, MLA_SLOT), lambda bb, j: (j, 0))
    in_specs = [_tok_spec(tm, d), mod.spec(0, is_ctx, 0), mod.spec(1, is_ctx, 0), _const_spec((1, d)),
                tab_spec, tab_spec] + [_const_spec(w.shape) for w in wts]
    out_q = qw if need_q else LANES
    return pl.pallas_call(
        functools.partial(_mla_proj_kernel, use_rope=use_rope, need_q=need_q),
        out_shape=(jax.ShapeDtypeStruct((b, l, out_q), BF16),
                   jax.ShapeDtypeStruct((b, l, qw), BF16),
                   jax.ShapeDtypeStruct((b, l, MLA_HEADS * MLA_V), BF16)),
        grid=(b, l // tm),
        in_specs=in_specs,
        out_specs=(_tok_spec(tm, out_q), _tok_spec(tm, qw), _tok_spec(tm, MLA_HEADS * MLA_V)),
        compiler_params=_cparams("parallel", "parallel"),
        name="mla_proj_ctx" if is_ctx else "mla_proj",
    )(x, mod.table, mod.table, norm_g, cos, sin, *wts)


def _qkv_proj_kernel(x_ref, sh_ref, sc_ref, g_ref, cos_ref, sin_ref, w_ref, gq_ref, gk_ref,
                     q_ref, k_ref, v_ref, *, n_q, n_kv, dh, use_rope, need_q, q_scale):
    h = _modulate(x_ref[...], g_ref[...], sh_ref[...], sc_ref[...]).astype(BF16)
    z = _dot(h, w_ref[...])
    q_cols, kv_cols = n_q * dh, n_kv * dh
    v_ref[...] = z[:, q_cols + kv_cols:].astype(BF16)
    lane = lax.broadcasted_iota(jnp.int32, (1, LANES), 1)
    if use_rope:
        cos, sin = cos_ref[...], sin_ref[...]
        lower = (lane % (dh // 2)) < (dh // 4)

    def normed(seg, gain):
        sq = seg * seg
        if dh == LANES:
            r = lax.rsqrt(jnp.sum(sq, axis=-1, keepdims=True) * (1.0 / dh) + EPS)
        else:
            first = lane < dh
            s_all = jnp.sum(sq, axis=-1, keepdims=True)
            s_first = jnp.sum(jnp.where(first, sq, 0.0), axis=-1, keepdims=True)
            r = lax.rsqrt(jnp.where(first, s_first, s_all - s_first) * (1.0 / dh) + EPS)
        out = seg * r * gain
        if use_rope:
            out = _rope_lanes(out, cos, sin, lower, dh // 4)
        return out.astype(BF16)

    gk = gk_ref[...]
    for s in range(kv_cols // LANES):
        k_ref[:, s * LANES:(s + 1) * LANES] = normed(z[:, q_cols + s * LANES:q_cols + (s + 1) * LANES], gk)
    if need_q:
        gq = gq_ref[...] * q_scale
        for s in range(q_cols // LANES):
            q_ref[:, s * LANES:(s + 1) * LANES] = normed(z[:, s * LANES:(s + 1) * LANES], gq)
    else:
        q_ref[...] = jnp.zeros_like(q_ref)


def _qkv_proj(x, mod, is_ctx, norm_g, tabs, w, gq, gk, n_q, n_kv, dh, tm, use_rope, need_q, name):
    b, l, d = x.shape
    cos, sin = tabs
    q_cols, kv_cols = n_q * dh, n_kv * dh
    if not need_q:
        w = w[:, q_cols:]
    tab_spec = pl.BlockSpec((tm, LANES), lambda bb, j: (j, 0))
    out_q = q_cols if need_q else LANES
    kern = functools.partial(_qkv_proj_kernel, n_q=n_q if need_q else 0, n_kv=n_kv, dh=dh,
                             use_rope=use_rope, need_q=need_q, q_scale=dh ** -0.5 * LOG2E)
    return pl.pallas_call(
        kern,
        out_shape=(jax.ShapeDtypeStruct((b, l, out_q), BF16),
                   jax.ShapeDtypeStruct((b, l, kv_cols), BF16),
                   jax.ShapeDtypeStruct((b, l, kv_cols), BF16)),
        grid=(b, l // tm),
        in_specs=[_tok_spec(tm, d), mod.spec(0, is_ctx, 0), mod.spec(1, is_ctx, 0), _const_spec((1, d)),
                  tab_spec, tab_spec, _const_spec(w.shape), _const_spec(gq.shape), _const_spec(gk.shape)],
        out_specs=(_tok_spec(tm, out_q), _tok_spec(tm, kv_cols), _tok_spec(tm, kv_cols)),
        compiler_params=_cparams("parallel", "parallel"),
        name=name,
    )(x, mod.table, mod.table, norm_g, cos, sin, w, gq, gk)


def _two_stage(n_tiles, score, finish):
    score(0, 0)
    if n_tiles == 1:
        finish(0, 0)
        return
    assert n_tiles % 2 == 0

    def body(j, carry):
        i = 2 * j
        score(i + 1, 1)
        finish(i, 0)
        score(i + 2, 0)
        finish(i + 1, 1)
        return carry

    lax.fori_loop(0, n_tiles // 2 - 1, body, 0)
    score(n_tiles - 1, 1)
    finish(n_tiles - 2, 0)
    finish(n_tiles - 1, 1)


def _store_scores(s_ref, m_ref, parts):
    m = None
    col = 0
    for s in parts:
        pm = s.max(axis=-1, keepdims=True)
        m = pm if m is None else jnp.maximum(m, pm)
        s_ref[:, col:col + s.shape[-1]] = s
        col += s.shape[-1]
    m_ref[...] = m


def _softmax_pv(s_ref, m_ref, v_parts):
    m = m_ref[...]
    denom = acc = None
    col = 0
    for v in v_parts:
        n = v.shape[0]
        p = jnp.exp2(s_ref[:, col:col + n] - m)
        col += n
        d = jnp.sum(p, axis=-1, keepdims=True)
        o = _dot(p.astype(BF16), v)
        denom = d if denom is None else denom + d
        acc = o if acc is None else acc + o
    return acc * (1.0 / denom)


def _attn_kernel(*refs, tq, heads, mode, has_latent):
    n_in = 5 if has_latent else 3
    q_ref, kc_ref, vc_ref = refs[:3]
    kl_ref, vl_ref = refs[3:5] if has_latent else (None, None)
    o_ref = refs[n_in]
    scratch = refs[n_in + 1:]
    s_bufs, m_bufs = scratch[:2 * heads], scratch[2 * heads:]
    n_tiles = q_ref.shape[0] // tq
    lane = lax.broadcasted_iota(jnp.int32, (1, LANES), 1)
    first = lane < (LANES // 2)

    def rows_of(i):
        start = i * tq
        return pl.ds(start if isinstance(start, int) else pl.multiple_of(start, tq), tq)

    def score(i, slot):
        rows = rows_of(i)
        for e in range(heads):
            if mode == "pair":
                q2 = q_ref[rows, :]
                q = jnp.where(first if e == 0 else jnp.logical_not(first), q2, jnp.zeros_like(q2))
            else:
                q = q_ref[rows, e * LANES:(e + 1) * LANES]
            ksl = slice(e * LANES, (e + 1) * LANES) if mode == "slots" else slice(0, LANES)
            parts = [_dot_nt(q, kc_ref[:, ksl])]
            if has_latent:
                parts.append(_dot_nt(q, kl_ref[:, ksl]))
            _store_scores(s_bufs[2 * e + slot], m_bufs[2 * e + slot], parts)

    def finish(i, slot):
        rows = rows_of(i)
        v_parts = [vc_ref[...]] + ([vl_ref[...]] if has_latent else [])
        outs = [_softmax_pv(s_bufs[2 * e + slot], m_bufs[2 * e + slot], v_parts) for e in range(heads)]
        if mode == "shared":
            for e in range(heads):
                o_ref[rows, e * LANES:(e + 1) * LANES] = outs[e].astype(o_ref.dtype)
        else:
            o_ref[rows, :] = jnp.where(first, outs[0], outs[1]).astype(o_ref.dtype)

    _two_stage(n_tiles, score, finish)


def _score_scratch(heads, tq, n_keys):
    return ([pltpu.VMEM((tq, n_keys), F32) for _ in range(2 * heads)]
            + [pltpu.VMEM((tq, 1), F32) for _ in range(2 * heads)])


def _attention(q, kc, vc, kl, vl, *, groups, q_lanes, k_lanes, heads, mode, tq, name):
    b, l, _ = q.shape
    c = kc.shape[1]
    has_latent = kl is not None
    tq = min(tq, l)
    n_keys = c + (kl.shape[1] if has_latent else 0)
    o_lanes = heads * LANES if mode == "shared" else LANES

    def gspec(n, w):
        return pl.BlockSpec((None, n, w), lambda bb, g: (bb, 0, g))

    in_specs = [gspec(l, q_lanes), gspec(c, k_lanes), gspec(c, LANES)]
    args = [q, kc, vc]
    if has_latent:
        s = kl.shape[1]
        in_specs += [gspec(s, k_lanes), gspec(s, LANES)]
        args += [kl, vl]
    return pl.pallas_call(
        functools.partial(_attn_kernel, tq=tq, heads=heads, mode=mode, has_latent=has_latent),
        out_shape=jax.ShapeDtypeStruct((b, l, groups * o_lanes), BF16),
        grid=(b, groups),
        in_specs=in_specs,
        out_specs=gspec(l, o_lanes),
        scratch_shapes=_score_scratch(heads, tq, n_keys),
        compiler_params=_cparams("parallel", "parallel"),
        name=name,
    )(*args)


def _na_kernel(q_ref, kc_ref, vc_ref, kl_ref, vl_ref, bias_ref, o_ref, *scratch, n_blocks):
    tq = NA_QROWS * GRID_W
    band = NA_BAND * GRID_W
    rows_total = n_blocks * NA_QROWS
    s_bufs, m_bufs = scratch[:4], scratch[4:]
    lane = lax.broadcasted_iota(jnp.int32, (1, LANES), 1)
    first = lane < (LANES // 2)

    def rows_of(i):
        start = i * tq
        return pl.ds(start if isinstance(start, int) else pl.multiple_of(start, tq), tq)

    def keys_of(i):
        if isinstance(i, int):
            return pl.ds(min(max(i * NA_QROWS - NA_WIN_H // 2, 0), rows_total - NA_BAND) * GRID_W, band)
        band_row0 = jnp.clip(i * NA_QROWS - NA_WIN_H // 2, 0, rows_total - NA_BAND)
        return pl.ds(pl.multiple_of(band_row0 * GRID_W, GRID_W), band)

    def variant_of(i):
        if isinstance(i, int):
            return 0 if i == 0 else (2 if i == n_blocks - 1 else 1)
        return jnp.where(i == 0, 0, jnp.where(i == n_blocks - 1, 2, 1))

    def score(i, slot):
        q2 = q_ref[rows_of(i), :]
        kb = kl_ref[keys_of(i), :]
        var = variant_of(i)
        for e in range(2):
            q = jnp.where(first if e == 0 else jnp.logical_not(first), q2, jnp.zeros_like(q2))
            parts = [_dot_nt(q, kc_ref[...]), _dot_nt(q, kb) + bias_ref[e, var]]
            _store_scores(s_bufs[2 * e + slot], m_bufs[2 * e + slot], parts)

    def finish(i, slot):
        v_parts = [vc_ref[...], vl_ref[keys_of(i), :]]
        outs = [_softmax_pv(s_bufs[2 * e + slot], m_bufs[2 * e + slot], v_parts) for e in range(2)]
        o_ref[rows_of(i), :] = jnp.where(first, outs[0], outs[1]).astype(o_ref.dtype)

    _two_stage(n_blocks, score, finish)


def _na_attention(q, kc, vc, kl, vl, bias):
    b, s, w = q.shape
    c = kc.shape[1]
    pairs = w // LANES
    n_blocks = s // (NA_QROWS * GRID_W)

    def gspec(n):
        return pl.BlockSpec((None, n, LANES), lambda g, bb: (bb, 0, g))

    return pl.pallas_call(
        functools.partial(_na_kernel, n_blocks=n_blocks),
        out_shape=jax.ShapeDtypeStruct((b, s, w), BF16),
        grid=(pairs, b),
        in_specs=[gspec(s), gspec(c), gspec(c), gspec(s), gspec(s),
                  pl.BlockSpec((2,) + bias.shape[1:], lambda g, bb: (g, 0, 0, 0))],
        out_specs=gspec(s),
        scratch_shapes=_score_scratch(2, NA_QROWS * GRID_W, c + NA_BAND * GRID_W),
        compiler_params=_cparams("parallel", "parallel"),
        name="na_attention",
    )(q, kc, vc, kl, vl, bias)


def _na_bias_table(rpb, rows):
    n_blocks = rows // NA_QROWS
    n_heads, n_rel = rpb.shape[0], 2 * NA_WIN_H - 1
    j = jnp.arange(GRID_W)
    c0 = jnp.clip(j - NA_WIN_W // 2, 0, GRID_W - NA_WIN_W)
    col_ok = (j[None, :] >= c0[:, None]) & (j[None, :] < c0[:, None] + NA_WIN_W)
    col_idx = jnp.clip(j[None, :] - j[:, None] + NA_WIN_W - 1, 0, 2 * NA_WIN_W - 2)
    blocks = jnp.where(col_ok[None, None], rpb.astype(F32)[:, :, col_idx] * LOG2E, NEG_INF)
    pad_lo, pad_hi = NA_QROWS, NA_BAND
    strip = jnp.pad(blocks.transpose(0, 2, 1, 3), ((0, 0), (0, 0), (pad_lo, pad_hi), (0, 0)),
                    constant_values=NEG_INF)
    key_row = jnp.arange(NA_BAND)
    tables = []
    for i in (0, 1, n_blocks - 1):
        band0 = min(max(i * NA_QROWS - NA_WIN_H // 2, 0), rows - NA_BAND)
        per_row = []
        for a in range(NA_QROWS):
            qr = i * NA_QROWS + a
            r0 = min(max(qr - NA_WIN_H // 2, 0), rows - NA_WIN_H)
            rel0 = band0 - qr + NA_WIN_H - 1 + pad_lo
            assert 0 <= rel0 and rel0 + NA_BAND <= n_rel + pad_lo + pad_hi
            window = strip[:, :, rel0:rel0 + NA_BAND, :]
            row_ok = (band0 + key_row >= r0) & (band0 + key_row < r0 + NA_WIN_H)
            per_row.append(jnp.where(row_ok[None, None, :, None], window, NEG_INF))
        tables.append(jnp.stack(per_row, axis=1).reshape(n_heads, NA_QROWS * GRID_W, NA_BAND * GRID_W))
    return jnp.stack(tables, axis=1)


def _s5_kernel(uc_ref, ul_ref, kc_ref, ws_ref, wo_ref, dec_ref, yc_ref, yl_ref, s_ref, h_ref, *, batch):
    half = 2 * S5_STATE
    lane = lax.broadcasted_iota(jnp.int32, (1, half), 1)
    fwd = lane < S5_STATE
    dec = dec_ref[...]
    d_re, d_im = dec[:, :half], dec[:, half:]

    def scan(u_ref, y_ref, n_chunks, init_re, init_im):
        rows = n_chunks * batch
        u = u_ref[...]
        s_ref[0:rows, :] = _dot(u, ws_ref[...])

        def step(i, carry):
            st_re, st_im = carry
            rf = pl.ds(pl.multiple_of(i * batch, batch), batch)
            rb = pl.ds(pl.multiple_of((n_chunks - 1 - i) * batch, batch), batch)
            h_ref[rf, 0:S5_STATE] = st_re[:, :S5_STATE]
            h_ref[rf, half:half + S5_STATE] = st_im[:, :S5_STATE]
            h_ref[rb, S5_STATE:half] = st_re[:, S5_STATE:]
            h_ref[rb, half + S5_STATE:] = st_im[:, S5_STATE:]
            in_re = jnp.where(fwd, s_ref[rf, 0:half], s_ref[rb, 0:half])
            in_im = jnp.where(fwd, s_ref[rf, half:], s_ref[rb, half:])
            return (d_re * st_re - d_im * st_im + in_re, d_re * st_im + d_im * st_re + in_im)

        st_re, st_im = lax.fori_loop(0, n_chunks, step, (init_re, init_im))
        y = _dot(u, kc_ref[...]) + _dot(h_ref[0:rows, :].astype(BF16), wo_ref[...])
        y_ref[...] = y.astype(y_ref.dtype)
        return st_re, st_im

    zero = jnp.zeros((batch, half), F32)
    ctx_re, ctx_im = scan(uc_ref, yc_ref, uc_ref.shape[0] // batch, zero, zero)
    scan(ul_ref, yl_ref, ul_ref.shape[0] // batch, ctx_re, ctx_im)


def _s5_scan(uc, ul, kc, ws, wo, dec, batch):
    groups, rows_c, width = uc.shape
    rows_l = ul.shape[1]

    def gspec(n, w):
        return pl.BlockSpec((None, n, w), lambda g: (g, 0, 0))

    return pl.pallas_call(
        functools.partial(_s5_kernel, batch=batch),
        out_shape=(jax.ShapeDtypeStruct((groups, rows_c, width), BF16),
                   jax.ShapeDtypeStruct((groups, rows_l, width), BF16)),
        grid=(groups,),
        in_specs=[gspec(rows_c, width), gspec(rows_l, width), gspec(width, width), gspec(width, width),
                  gspec(width, width), gspec(1, width)],
        out_specs=(gspec(rows_c, width), gspec(rows_l, width)),
        scratch_shapes=[pltpu.VMEM((rows_l, width), F32), pltpu.VMEM((rows_l, width), F32)],
        compiler_params=_cparams("parallel"),
        name="s5_scan",
    )(uc, ul, kc, ws, wo, dec)


def _s5_matrices(a_re, a_im, log_dt, b_re, b_im, c_re, c_im):
    t = S5_CHUNK
    dt = jnp.exp(log_dt.astype(F32))[..., None]
    a_re, a_im = a_re.astype(F32), a_im.astype(F32)
    lam_re, lam_im = dt * a_re, dt * a_im

    def power(j):
        jj = jnp.asarray(j, F32)
        mag = jnp.exp(lam_re[..., None] * jj)
        return mag * jnp.cos(lam_im[..., None] * jj), mag * jnp.sin(lam_im[..., None] * jj)

    ab_re, ab_im = power(jnp.ones((1,)))
    ab_re, ab_im = ab_re[..., 0], ab_im[..., 0]
    den = a_re * a_re + a_im * a_im
    nr = ab_re - 1.0
    f_re = (nr * a_re + ab_im * a_im) / den
    f_im = (ab_im * a_re - nr * a_im) / den
    bb_re = f_re[..., None] * b_re - f_im[..., None] * b_im
    bb_im = f_re[..., None] * b_im + f_im[..., None] * b_re
    c_re, c_im = c_re.astype(F32), c_im.astype(F32)

    hi = lax.Precision.HIGHEST
    lags = jnp.arange(t)
    pw_re, pw_im = power(lags)
    cr, ci = c_re[:, :, :, None, :], c_im[:, :, :, None, :]
    pr, pi = pw_re.transpose(0, 1, 3, 2)[:, :, None], pw_im.transpose(0, 1, 3, 2)[:, :, None]
    cb_re = cr * pr - ci * pi
    cb_im = cr * pi + ci * pr
    conv = (jnp.einsum('dgclp,dgpi->dglic', cb_re, bb_re, precision=hi)
            - jnp.einsum('dgclp,dgpi->dglic', cb_im, bb_im, precision=hi))
    src, dst = lags[:, None], lags[None, :]
    lag_f = dst - src
    k_f = jnp.where((lag_f >= 0)[None, :, None, :, None], conv[0][:, jnp.clip(lag_f, 0, t - 1)].transpose(0, 1, 3, 2, 4), 0.0)
    lag_b = src - dst
    k_b = jnp.where((lag_b >= 0)[None, :, None, :, None], conv[1][:, jnp.clip(lag_b, 0, t - 1)].transpose(0, 1, 3, 2, 4), 0.0)
    g = a_re.shape[1]
    cg = b_re.shape[-1]
    k_mat = (k_f + k_b).reshape(g, t * cg, t * cg)

    def state_w(d, expo):
        p_re, p_im = power(expo)
        w_re = p_re[d][..., None] * bb_re[d][:, :, None, :] - p_im[d][..., None] * bb_im[d][:, :, None, :]
        w_im = p_re[d][..., None] * bb_im[d][:, :, None, :] + p_im[d][..., None] * bb_re[d][:, :, None, :]
        to_rows = lambda w: w.transpose(0, 2, 3, 1).reshape(g, t * cg, -1)
        return to_rows(w_re), to_rows(w_im)

    wf_re, wf_im = state_w(0, t - 1 - lags)
    wb_re, wb_im = state_w(1, lags)
    w_state = jnp.concatenate([wf_re, wb_re, wf_im, wb_im], axis=-1)

    def out_w(d, expo):
        p_re, p_im = power(expo)
        m_re = c_re[d][:, :, :, None] * p_re[d][:, None, :, :] - c_im[d][:, :, :, None] * p_im[d][:, None, :, :]
        m_im = c_re[d][:, :, :, None] * p_im[d][:, None, :, :] + c_im[d][:, :, :, None] * p_re[d][:, None, :, :]
        to_cols = lambda w: w.transpose(0, 2, 3, 1).reshape(g, -1, t * cg)
        return to_cols(m_re), -to_cols(m_im)

    of_re, of_im = out_w(0, lags + 1)
    ob_re, ob_im = out_w(1, t - lags)
    w_out = jnp.concatenate([of_re, ob_re, of_im, ob_im], axis=1)

    dT_re, dT_im = power(jnp.full((1,), float(t)))
    dT_re, dT_im = dT_re[..., 0], dT_im[..., 0]
    decay = jnp.concatenate([dT_re[0], dT_re[1], dT_im[0], dT_im[1]], axis=-1)[:, None, :]
    return k_mat.astype(BF16), w_state.astype(BF16), w_out.astype(BF16), decay


def _s5_proj_kernel(x_ref, sh_ref, sc_ref, g_ref, u_ref):
    u_ref[...] = _modulate(x_ref[...], g_ref[...], sh_ref[...], sc_ref[...]).astype(BF16)


def _s5_proj(x, mod, is_ctx, norm_g, tm):
    b, l, d = x.shape
    return pl.pallas_call(
        _s5_proj_kernel,
        out_shape=jax.ShapeDtypeStruct((b, l, d), BF16),
        grid=(b, l // tm),
        in_specs=[_tok_spec(tm, d), mod.spec(0, is_ctx, 0), mod.spec(1, is_ctx, 0), _const_spec((1, d))],
        out_specs=_tok_spec(tm, d),
        compiler_params=_cparams("parallel", "parallel"),
        name="s5_proj",
    )(x, mod.table, mod.table, norm_g)


def _to_chunks(u):
    b, l, d = u.shape
    g = d // S5_GROUP
    u = u.reshape(b, l // S5_CHUNK, S5_CHUNK, g, S5_GROUP).transpose(3, 1, 0, 2, 4)
    return u.reshape(g, (l // S5_CHUNK) * b, S5_CHUNK * S5_GROUP)


def _from_chunks(y, b):
    g, rows, _ = y.shape
    n = rows // b
    y = y.reshape(g, n, b, S5_CHUNK, S5_GROUP).transpose(2, 1, 3, 0, 4)
    return y.reshape(b, n * S5_CHUNK, g * S5_GROUP)


def _post_ffn_kernel(x_ref, y_ref, m_ref, gmix_ref, gffn_ref, dskip_ref, w_mix_ref, w_in_ref, w_out_ref,
                     o_ref, acc_ref, *, glu, hidden, chunk):
    sh, sc, gt, sh2, sc2, gt2 = (m_ref[i:i + 1, :] for i in range(6))
    x = x_ref[...]
    if glu:
        yv = dskip_ref[...] * _modulate(x, gmix_ref[...], sh, sc) + y_ref[...].astype(F32)
        ab = _dot(_gelu_tanh(yv).astype(BF16), w_mix_ref[...])
        n = ab.shape[-1] // 2
        mix = ab[:, :n] * jax.nn.sigmoid(ab[:, n:])
    else:
        mix = _dot(y_ref[...], w_mix_ref[...])
    x = x + gt * mix
    h = _modulate(x, gffn_ref[...], sh2, sc2).astype(BF16)
    for c0 in range(0, hidden, chunk):
        a = _dot(h, w_in_ref[:, c0:c0 + chunk])
        b = _dot(h, w_in_ref[:, hidden + c0:hidden + c0 + chunk])
        part = _dot((_silu(a) * b).astype(BF16), w_out_ref[c0:c0 + chunk, :])
        if c0 == 0:
            acc_ref[...] = part
        else:
            acc_ref[...] += part
    o_ref[...] = x + gt2 * acc_ref[...]


def _post_ffn(x, y, mod, is_ctx, g_mix, g_ffn, d_skip, w_mix, w_in, w_out, glu, tm):
    b, l, d = x.shape
    hidden = w_out.shape[0]
    ctx_row = mod.ctx_row
    if is_ctx:
        mspec = pl.BlockSpec((None, 6, d), lambda bb, j: (ctx_row, 0, 0))
    else:
        mspec = pl.BlockSpec((None, 6, d), lambda bb, j: (bb, 0, 0))
    single = dict(pipeline_mode=pl.Buffered(1))

    def wspec(shape):
        nd = len(shape)
        return pl.BlockSpec(shape, lambda *ids: (0,) * nd, **single)

    return pl.pallas_call(
        functools.partial(_post_ffn_kernel, glu=glu, hidden=hidden, chunk=hidden // 2),
        out_shape=jax.ShapeDtypeStruct((b, l, d), F32),
        grid=(b, l // tm),
        in_specs=[_tok_spec(tm, d), _tok_spec(tm, y.shape[-1]), mspec, _const_spec((1, d)), _const_spec((1, d)),
                  _const_spec((1, d)), wspec(w_mix.shape), wspec(w_in.shape), wspec(w_out.shape)],
        out_specs=_tok_spec(tm, d),
        scratch_shapes=[pltpu.VMEM((tm, d), F32)],
        compiler_params=_cparams("parallel", "parallel"),
        name="post_ffn_ctx" if is_ctx else "post_ffn",
    )(x, y, mod.table.reshape(mod.table.shape[0], 6, d), g_mix, g_ffn, d_skip, w_mix, w_in, w_out)


def _rope_tables(n_tokens, rot_dim, offset, width):
    t = jnp.arange(n_tokens, dtype=jnp.int32)
    axis_dim = rot_dim // 2
    inv_freq = ROPE_THETA ** (-jnp.arange(0, axis_dim, 2, dtype=F32) / axis_dim)
    lane = jnp.arange(width)
    j = lane - offset
    in_rot = (j >= 0) & (j < rot_dim)
    jc = jnp.clip(j, 0, rot_dim - 1)
    use_col = jc >= axis_dim
    within = jc % axis_dim
    freq = inv_freq[within % (axis_dim // 2)]
    lower = within < (axis_dim // 2)
    pos = jnp.where(use_col[None, :], (t % GRID_W)[:, None], (t // GRID_W)[:, None]).astype(F32)
    ang = pos * freq[None, :]
    cos = jnp.where(in_rot[None, :], jnp.cos(ang), 1.0)
    sin = jnp.where(in_rot[None, :], jnp.where(lower[None, :], -jnp.sin(ang), jnp.sin(ang)), 0.0)
    return cos, sin


def _mla_weights(w_in, g_q, g_kv, w_uq, w_ukv, g_qn, g_kn):
    d = w_in.shape[0]
    nh, slot = MLA_HEADS, MLA_SLOT
    rope0 = MLA_Q_LORA + MLA_KV_LORA
    w_rope = w_in[:, rope0:]
    quarter = MLA_ROPE // 4
    partner = jnp.arange(MLA_ROPE).reshape(-1, 2, quarter)[:, ::-1, :].reshape(-1)

    def slotted(cols):
        return jnp.pad(cols, ((0, 0), (MLA_NOPE, slot - MLA_QK)))

    w_in_p = jnp.concatenate([w_in[:, :rope0], slotted(w_rope), slotted(w_rope[:, partner])], axis=1)
    w_uq_p = jnp.pad(w_uq.reshape(-1, nh, MLA_QK), ((0, 0), (0, 0), (0, slot - MLA_QK))).reshape(-1, nh * slot)
    kv = w_ukv.reshape(-1, nh, MLA_NOPE + MLA_V)
    w_k = jnp.pad(kv[:, :, :MLA_NOPE], ((0, 0), (0, 0), (0, slot - MLA_NOPE))).reshape(-1, nh * slot)
    w_v = kv[:, :, MLA_NOPE:].reshape(-1, nh * MLA_V)
    w_ukv_p = jnp.concatenate([w_k, w_v], axis=1)
    pad_slot = lambda g: jnp.pad(g, (0, slot - g.shape[0]))[None, :]
    gqn = pad_slot(g_qn)
    gkn_nope = pad_slot(g_kn[:MLA_NOPE])
    gkr = jnp.pad(g_kn[MLA_NOPE:], (MLA_NOPE, slot - MLA_QK))[None, :]
    gkrs = jnp.pad(g_kn[MLA_NOPE:][partner], (MLA_NOPE, slot - MLA_QK))[None, :]
    return (w_in_p.astype(BF16), g_q[None, :], g_kv[None, :], w_uq_p.astype(BF16), w_ukv_p.astype(BF16),
            gqn, gkn_nope, gkr, gkrs)


def _tile_gain(g):
    return jnp.tile(g, LANES // g.shape[0])[None, :]


def kernel(x, c, ctx, c_ctx, ada_w, ada_b, norm_mix, norm_ffn, ffn_w_in, ffn_w_out,
           mla_w_in, mla_g_q, mla_g_kv, mla_w_uq, mla_w_ukv, mla_g_qn, mla_g_kn, mla_w_o,
           s5_a_re, s5_a_im, s5_log_dt, s5_b_re, s5_b_im, s5_c_re, s5_c_im, s5_d, s5_w_glu,
           na_w_qkv, na_g_qn, na_g_kn, na_rpb, na_w_o,
           gqa_w_qkv, gqa_g_qn, gqa_g_kn, gqa_w_o):
    batch, seq, d = x.shape
    n_ctx = ctx.shape[1]
    depth = ada_w.shape[0]
    n_mixers = 4
    ctx_row = batch
    mod_rows = -(-(batch + 1) // 8) * 8
    cvec = jnp.concatenate([c, c_ctx[None, :], jnp.zeros((mod_rows - batch - 1, d), F32)], axis=0)
    ada = _ada_terms(cvec, ada_w, ada_b).reshape(depth, mod_rows, 1, 6 * d)

    mla_tabs = _rope_tables(seq, MLA_ROPE, MLA_NOPE, MLA_SLOT)
    gqa_dh = d // GQA_HEADS
    gqa_tabs = _rope_tables(seq, gqa_dh, 0, LANES)
    ident_tabs = (jnp.ones((n_ctx, LANES), F32), jnp.zeros((n_ctx, LANES), F32))
    ident_tabs_l = (jnp.ones((seq, LANES), F32), jnp.zeros((seq, LANES), F32))
    zero_skip = jnp.zeros((1, d), F32)
    tm_l, tm_c = 512, 256

    xl, xc = x, ctx
    for i in range(depth):
        kind, j = i % n_mixers, i // n_mixers
        ctx_out = i < depth - 1
        mod = _Mod(ada[i], d, ctx_row)
        g_mix, g_ffn = norm_mix[i][None, :], norm_ffn[i][None, :]
        glu = False
        d_skip = zero_skip
        if kind == 0:
            wts = _mla_weights(mla_w_in[j], mla_g_q[j], mla_g_kv[j], mla_w_uq[j], mla_w_ukv[j],
                               mla_g_qn[j], mla_g_kn[j])
            qc, kc, vc = _mla_proj(xc, mod, True, g_mix, ident_tabs, wts, tm_c, need_q=ctx_out)
            ql, kl, vl = _mla_proj(xl, mod, False, g_mix, mla_tabs, wts, tm_c)
            cfg = dict(groups=MLA_HEADS // 2, q_lanes=2 * MLA_SLOT, k_lanes=2 * MLA_SLOT, heads=2,
                       mode="slots", tq=256)
            yl = _attention(ql, kc, vc, kl, vl, name="mla_attn", **cfg)
            yc = _attention(qc, kc, vc, None, None, name="mla_attn_ctx", **cfg) if ctx_out else None
            w_mix = mla_w_o[j].astype(BF16)
        elif kind == 1:
            uc = _to_chunks(_s5_proj(xc, mod, True, g_mix, tm_c))
            ul = _to_chunks(_s5_proj(xl, mod, False, g_mix, tm_l))
            mats = _s5_matrices(s5_a_re[j], s5_a_im[j], s5_log_dt[j], s5_b_re[j], s5_b_im[j],
                                s5_c_re[j], s5_c_im[j])
            yc, yl = _s5_scan(uc, ul, *mats, batch)
            yc, yl = _from_chunks(yc, batch), _from_chunks(yl, batch)
            w_mix = s5_w_glu[j].astype(BF16)
            glu = True
            d_skip = s5_d[j][None, :]
        elif kind == 2:
            dh = d // NA_HEADS
            w = na_w_qkv[j].astype(BF16)
            gq, gk = _tile_gain(na_g_qn[j]), _tile_gain(na_g_kn[j])
            qc, kc, vc = _qkv_proj(xc, mod, True, g_mix, ident_tabs, w, gq, gk, NA_HEADS, NA_HEADS, dh, tm_c,
                                   False, ctx_out, "na_proj_ctx")
            ql, kl, vl = _qkv_proj(xl, mod, False, g_mix, ident_tabs_l, w, gq, gk, NA_HEADS, NA_HEADS, dh, tm_l,
                                   False, True, "na_proj")
            bias = _na_bias_table(na_rpb[j], seq // GRID_W)
            yl = _na_attention(ql, kc, vc, kl, vl, bias)
            if ctx_out:
                yc = _attention(qc, kc, vc, None, None, groups=NA_HEADS // 2, q_lanes=LANES, k_lanes=LANES,
                                heads=2, mode="pair", tq=256, name="na_attn_ctx")
            w_mix = na_w_o[j].astype(BF16)
        else:
            w = gqa_w_qkv[j].astype(BF16)
            gq, gk = gqa_g_qn[j][None, :], gqa_g_kn[j][None, :]
            per_kv = GQA_HEADS // GQA_KV_HEADS
            qc, kc, vc = _qkv_proj(xc, mod, True, g_mix, ident_tabs, w, gq, gk, GQA_HEADS, GQA_KV_HEADS, gqa_dh,
                                   tm_c, False, ctx_out, "gqa_proj_ctx")
            ql, kl, vl = _qkv_proj(xl, mod, False, g_mix, gqa_tabs, w, gq, gk, GQA_HEADS, GQA_KV_HEADS, gqa_dh,
                                   tm_l, True, True, "gqa_proj")
            cfg = dict(groups=GQA_KV_HEADS, q_lanes=per_kv * LANES, k_lanes=LANES, heads=per_kv, mode="shared",
                       tq=128)
            yl = _attention(ql, kc, vc, kl, vl, name="gqa_attn", **cfg)
            yc = _attention(qc, kc, vc, None, None, name="gqa_attn_ctx", **cfg) if ctx_out else None
            w_mix = gqa_w_o[j].astype(BF16)

        w_in, w_out = ffn_w_in[i].astype(BF16), ffn_w_out[i].astype(BF16)
        xl = _post_ffn(xl, yl, mod, False, g_mix, g_ffn, d_skip, w_mix, w_in, w_out, glu, tm_l)
        if ctx_out:
            xc = _post_ffn(xc, yc, mod, True, g_mix, g_ffn, d_skip, w_mix, w_in, w_out, glu, tm_c)
    return xl
```

```python
import functools
import math

import jax
import jax.numpy as jnp
from jax import lax
from jax.experimental import pallas as pl
from jax.experimental.pallas import tpu as pltpu

F32 = jnp.float32
BF16 = jnp.bfloat16

EPS = 1e-6
ROPE_THETA = 10000.0
GRID_W = 64
NEG_INF = -1e30
LOG2E = math.log2(math.e)

MLA_HEADS = 16
MLA_NOPE = 64
MLA_ROPE = 32
MLA_QK = MLA_NOPE + MLA_ROPE
MLA_V = 64
MLA_Q_LORA = 384
MLA_KV_LORA = 256
MLA_SLOT = 128

S5_GROUP = 16
S5_STATE = 64
S5_CHUNK = 16

NA_HEADS = 16
NA_WIN_H = 8
NA_WIN_W = 16
NA_QROWS = 4
NA_BAND = NA_QROWS + NA_WIN_H

GQA_HEADS = 8
GQA_KV_HEADS = 2

LANES = 128
KEY_CHUNK = 256
VMEM_LIMIT = 56 * 1024 * 1024


def _cparams(*sem):
    return pltpu.CompilerParams(dimension_semantics=sem, vmem_limit_bytes=VMEM_LIMIT)


def _modulate(x, g, shift, scale):
    ms = jnp.mean(x * x, axis=-1, keepdims=True)
    return x * lax.rsqrt(ms + EPS) * g * (1.0 + scale) + shift


def _rms_rows(x, n):
    ss = jnp.sum(x * x, axis=-1, keepdims=True)
    return lax.rsqrt(ss * (1.0 / n) + EPS)


def _silu(x):
    return x * jax.nn.sigmoid(x)


def _gelu_tanh(x):
    return 0.5 * x * (1.0 + jnp.tanh(math.sqrt(2.0 / math.pi) * (x + 0.044715 * x * x * x)))


def _rope_lanes(x, cos, sin_signed, lower, shift):
    width = x.shape[-1]
    partner = jnp.where(lower, pltpu.roll(x, width - shift, 1), pltpu.roll(x, shift, 1))
    return x * cos + partner * sin_signed


def _dot(a, b):
    return jnp.dot(a, b, preferred_element_type=F32)


def _dot_nt(a, b):
    return lax.dot_general(a, b, (((1,), (1,)), ((), ())), preferred_element_type=F32)


def _ada_kernel(c_ref, w_ref, b_ref, o_ref):
    c = c_ref[...]
    o_ref[...] = _dot(_silu(c), w_ref[...]) + b_ref[...]


def _ada_terms(cvec, ada_w, ada_b):
    depth, d, n6 = ada_w.shape
    rows = cvec.shape[0]
    tn = 1536
    return pl.pallas_call(
        _ada_kernel,
        out_shape=jax.ShapeDtypeStruct((depth, rows, n6), F32),
        grid=(depth, n6 // tn),
        in_specs=[
            pl.BlockSpec((rows, d), lambda i, j: (0, 0)),
            pl.BlockSpec((None, d, tn), lambda i, j: (i, 0, j)),
            pl.BlockSpec((None, 1, tn), lambda i, j: (i, 0, j)),
        ],
        out_specs=pl.BlockSpec((None, rows, tn), lambda i, j: (i, 0, j)),
        compiler_params=_cparams("parallel", "parallel"),
        name="ada_terms",
    )(cvec, ada_w, ada_b.reshape(depth, 1, n6))


class _Mod:
    def __init__(self, table, d, ctx_row):
        self.table, self.d, self.ctx_row = table, d, ctx_row

    def spec(self, term, is_ctx, batch_axis):
        ctx_row = self.ctx_row
        if is_ctx:
            return pl.BlockSpec((None, 1, self.d), lambda *ids: (ctx_row, 0, term))
        return pl.BlockSpec((None, 1, self.d), lambda *ids: (ids[batch_axis], 0, term))


def _const_spec(shape):
    nd = len(shape)
    return pl.BlockSpec(shape, lambda *ids: (0,) * nd)


def _tok_spec(tm, width):
    return pl.BlockSpec((None, tm, width), lambda b, j: (b, j, 0))


def _mla_proj_kernel(x_ref, sh_ref, sc_ref, g_ref, cos_ref, sin_ref, w_in_ref, gq_ref, gkv_ref,
                     w_uq_ref, w_ukv_ref, gqn_ref, gkn_ref, gkr_ref, gkrs_ref,
                     q_ref, k_ref, v_ref, *, use_rope, need_q):
    h = _modulate(x_ref[...], g_ref[...], sh_ref[...], sc_ref[...]).astype(BF16)
    z = _dot(h, w_in_ref[...])
    lane = lax.broadcasted_iota(jnp.int32, (1, MLA_SLOT), 1)
    lower = ((lane - MLA_NOPE) % (MLA_ROPE // 2)) < (MLA_ROPE // 4)
    if use_rope:
        cos, sin = cos_ref[...], sin_ref[...]

    kv0 = MLA_Q_LORA
    zkv = z[:, kv0:kv0 + MLA_KV_LORA]
    ckv = (zkv * _rms_rows(zkv, MLA_KV_LORA) * gkv_ref[...]).astype(BF16)
    kv = _dot(ckv, w_ukv_ref[...])
    v_ref[...] = kv[:, MLA_HEADS * MLA_SLOT:].astype(BF16)
    r0 = kv0 + MLA_KV_LORA
    zr = z[:, r0:r0 + MLA_SLOT]
    zrs = z[:, r0 + MLA_SLOT:r0 + 2 * MLA_SLOT]
    ss_rope = jnp.sum(zr * zr, axis=-1, keepdims=True)
    if use_rope:
        kr = zr * gkr_ref[...] * cos + zrs * gkrs_ref[...] * sin
    else:
        kr = zr * gkr_ref[...]
    gkn = gkn_ref[...]
    for hd in range(MLA_HEADS):
        seg = kv[:, hd * MLA_SLOT:(hd + 1) * MLA_SLOT]
        ss = jnp.sum(seg * seg, axis=-1, keepdims=True) + ss_rope
        r = lax.rsqrt(ss * (1.0 / MLA_QK) + EPS)
        k_ref[:, hd * MLA_SLOT:(hd + 1) * MLA_SLOT] = ((seg * gkn + kr) * r).astype(BF16)

    if need_q:
        zq = z[:, :MLA_Q_LORA]
        cq = (zq * _rms_rows(zq, MLA_Q_LORA) * gq_ref[...]).astype(BF16)
        qp = _dot(cq, w_uq_ref[...])
        gqn = gqn_ref[...] * (MLA_QK ** -0.5 * LOG2E)
        for hd in range(MLA_HEADS):
            seg = qp[:, hd * MLA_SLOT:(hd + 1) * MLA_SLOT]
            qn = seg * _rms_rows(seg, MLA_QK) * gqn
            if use_rope:
                qn = _rope_lanes(qn, cos, sin, lower, MLA_ROPE // 4)
            q_ref[:, hd * MLA_SLOT:(hd + 1) * MLA_SLOT] = qn.astype(BF16)
    else:
        q_ref[...] = jnp.zeros_like(q_ref)


def _mla_proj(x, mod, is_ctx, norm_g, tabs, wts, tm, need_q=True):
    b, l, d = x.shape
    use_rope = not is_ctx
    cos, sin = tabs
    qw = MLA_HEADS * MLA_SLOT
    tab_spec = pl.BlockSpec((tm, MLA_SLOT), lambda bb, j: (j, 0))
    in_specs = [_tok_spec(tm, d), mod.spec(0, is_ctx, 0), mod.spec(1, is_ctx, 0), _const_spec((1, d)),
                tab_spec, tab_spec] + [_const_spec(w.shape) for w in wts]
    out_q = qw if need_q else LANES
    return pl.pallas_call(
        functools.partial(_mla_proj_kernel, use_rope=use_rope, need_q=need_q),
        out_shape=(jax.ShapeDtypeStruct((b, l, out_q), BF16),
                   jax.ShapeDtypeStruct((b, l, qw), BF16),
                   jax.ShapeDtypeStruct((b, l, MLA_HEADS * MLA_V), BF16)),
        grid=(b, l // tm),
        in_specs=in_specs,
        out_specs=(_tok_spec(tm, out_q), _tok_spec(tm, qw), _tok_spec(tm, MLA_HEADS * MLA_V)),
        compiler_params=_cparams("parallel", "parallel"),
        name="mla_proj_ctx" if is_ctx else "mla_proj",
    )(x, mod.table, mod.table, norm_g, cos, sin, *wts)


def _qkv_proj_kernel(x_ref, sh_ref, sc_ref, g_ref, cos_ref, sin_ref, w_ref, gq_ref, gk_ref,
                     q_ref, k_ref, v_ref, *, n_q, n_kv, dh, use_rope, need_q, q_scale):
    h = _modulate(x_ref[...], g_ref[...], sh_ref[...], sc_ref[...]).astype(BF16)
    z = _dot(h, w_ref[...])
    q_cols, kv_cols = n_q * dh, n_kv * dh
    v_ref[...] = z[:, q_cols + kv_cols:].astype(BF16)
    lane = lax.broadcasted_iota(jnp.int32, (1, LANES), 1)
    if use_rope:
        cos, sin = cos_ref[...], sin_ref[...]
        lower = (lane % (dh // 2)) < (dh // 4)

    def normed(seg, gain):
        sq = seg * seg
        if dh == LANES:
            r = lax.rsqrt(jnp.sum(sq, axis=-1, keepdims=True) * (1.0 / dh) + EPS)
        else:
            first = lane < dh
            s_all = jnp.sum(sq, axis=-1, keepdims=True)
            s_first = jnp.sum(jnp.where(first, sq, 0.0), axis=-1, keepdims=True)
            r = lax.rsqrt(jnp.where(first, s_first, s_all - s_first) * (1.0 / dh) + EPS)
        out = seg * r * gain
        if use_rope:
            out = _rope_lanes(out, cos, sin, lower, dh // 4)
        return out.astype(BF16)

    gk = gk_ref[...]
    for s in range(kv_cols // LANES):
        k_ref[:, s * LANES:(s + 1) * LANES] = normed(z[:, q_cols + s * LANES:q_cols + (s + 1) * LANES], gk)
    if need_q:
        gq = gq_ref[...] * q_scale
        for s in range(q_cols // LANES):
            q_ref[:, s * LANES:(s + 1) * LANES] = normed(z[:, s * LANES:(s + 1) * LANES], gq)
    else:
        q_ref[...] = jnp.zeros_like(q_ref)


def _qkv_proj(x, mod, is_ctx, norm_g, tabs, w, gq, gk, n_q, n_kv, dh, tm, use_rope, need_q, name):
    b, l, d = x.shape
    cos, sin = tabs
    q_cols, kv_cols = n_q * dh, n_kv * dh
    if not need_q:
        w = w[:, q_cols:]
    tab_spec = pl.BlockSpec((tm, LANES), lambda bb, j: (j, 0))
    out_q = q_cols if need_q else LANES
    kern = functools.partial(_qkv_proj_kernel, n_q=n_q if need_q else 0, n_kv=n_kv, dh=dh,
                             use_rope=use_rope, need_q=need_q, q_scale=dh ** -0.5 * LOG2E)
    return pl.pallas_call(
        kern,
        out_shape=(jax.ShapeDtypeStruct((b, l, out_q), BF16),
                   jax.ShapeDtypeStruct((b, l, kv_cols), BF16),
                   jax.ShapeDtypeStruct((b, l, kv_cols), BF16)),
        grid=(b, l // tm),
        in_specs=[_tok_spec(tm, d), mod.spec(0, is_ctx, 0), mod.spec(1, is_ctx, 0), _const_spec((1, d)),
                  tab_spec, tab_spec, _const_spec(w.shape), _const_spec(gq.shape), _const_spec(gk.shape)],
        out_specs=(_tok_spec(tm, out_q), _tok_spec(tm, kv_cols), _tok_spec(tm, kv_cols)),
        compiler_params=_cparams("parallel", "parallel"),
        name=name,
    )(x, mod.table, mod.table, norm_g, cos, sin, w, gq, gk)


def _two_stage(n_tiles, score, finish):
    def run(*stages):
        stages = list(stages)
        while stages:
            for g in list(stages):
                try:
                    next(g)
                except StopIteration:
                    stages.remove(g)

    run(score(0, 0))
    if n_tiles == 1:
        run(finish(0, 0))
        return
    assert n_tiles % 2 == 0

    def body(j, carry):
        i = 2 * j
        run(score(i + 1, 1), finish(i, 0))
        run(score(i + 2, 0), finish(i + 1, 1))
        return carry

    lax.fori_loop(0, n_tiles // 2 - 1, body, 0)
    run(score(n_tiles - 1, 1), finish(n_tiles - 2, 0))
    run(finish(n_tiles - 1, 1))


def _attn_stages(n_chunks, n_chains, s_bufs, m_bufs, load_q, key_chunk, val_chunk, bias_chunk, emit):
    half = KEY_CHUNK // 2

    def score(i, slot):
        qs = [load_q(i, ch) for ch in range(n_chains)]
        m_run = [None] * n_chains
        for c in range(n_chunks):
            for ch in range(n_chains):
                s = _dot_nt(qs[ch], key_chunk(i, ch, c))
                bias = bias_chunk(i, ch, c)
                if bias is not None:
                    s = s + bias
                pm = jnp.maximum(s[:, :half], s[:, half:])
                m_run[ch] = pm if m_run[ch] is None else jnp.maximum(m_run[ch], pm)
                s_bufs[ch][slot][:, c * KEY_CHUNK:(c + 1) * KEY_CHUNK] = s
            yield
        for ch in range(n_chains):
            m_bufs[ch][slot][...] = m_run[ch].max(axis=-1, keepdims=True)

    def finish(i, slot):
        ms = [m_bufs[ch][slot][...] for ch in range(n_chains)]
        l_run = [None] * n_chains
        acc = [None] * n_chains
        for c in range(n_chunks):
            for ch in range(n_chains):
                p = jnp.exp2(s_bufs[ch][slot][:, c * KEY_CHUNK:(c + 1) * KEY_CHUNK] - ms[ch])
                ps = p[:, :half] + p[:, half:]
                o = _dot(p.astype(BF16), val_chunk(i, ch, c))
                l_run[ch] = ps if l_run[ch] is None else l_run[ch] + ps
                acc[ch] = o if acc[ch] is None else acc[ch] + o
            yield
        emit(i, [acc[ch] * (1.0 / l_run[ch].sum(axis=-1, keepdims=True)) for ch in range(n_chains)])

    return score, finish


def _attn_kernel(*refs, tq, heads, mode, has_latent):
    n_in = 5 if has_latent else 3
    q_ref, kc_ref, vc_ref = refs[:3]
    kl_ref, vl_ref = refs[3:5] if has_latent else (None, None)
    o_ref = refs[n_in]
    n_chains = 2 if mode == "slots" else 1
    s_bufs, m_bufs = _split_scratch(refs[n_in + 1:], n_chains)
    n_tiles = q_ref.shape[0] // tq
    ctx_chunks = kc_ref.shape[0] // KEY_CHUNK
    n_chunks = ctx_chunks + (kl_ref.shape[0] // KEY_CHUNK if has_latent else 0)
    lane = lax.broadcasted_iota(jnp.int32, (1, LANES), 1)
    first = lane < (LANES // 2)

    def rows_of(i):
        start = i * tq
        return pl.ds(start if isinstance(start, int) else pl.multiple_of(start, tq), tq)

    def load_q(i, ch):
        rows = rows_of(i)
        if mode == "slots":
            return q_ref[rows, ch * LANES:(ch + 1) * LANES]
        if mode == "shared":
            return jnp.concatenate([q_ref[rows, e * LANES:(e + 1) * LANES] for e in range(heads)], axis=0)
        return _split_pair(q_ref[rows, :], first)

    def chunk_of(ctx_ref, lat_ref, ch, c, lanes):
        ref, c0 = (ctx_ref, c) if c < ctx_chunks else (lat_ref, c - ctx_chunks)
        return ref[c0 * KEY_CHUNK:(c0 + 1) * KEY_CHUNK, lanes]

    def key_chunk(i, ch, c):
        lanes = slice(ch * LANES, (ch + 1) * LANES) if mode == "slots" else slice(0, LANES)
        return chunk_of(kc_ref, kl_ref, ch, c, lanes)

    def val_chunk(i, ch, c):
        return chunk_of(vc_ref, vl_ref, ch, c, slice(0, LANES))

    def emit(i, outs):
        rows = rows_of(i)
        if mode == "slots":
            o_ref[rows, :] = jnp.where(first, outs[0], outs[1]).astype(o_ref.dtype)
        elif mode == "shared":
            for e in range(heads):
                o_ref[rows, e * LANES:(e + 1) * LANES] = outs[0][e * tq:(e + 1) * tq].astype(o_ref.dtype)
        else:
            o_ref[rows, :] = jnp.where(first, outs[0][:tq], outs[0][tq:]).astype(o_ref.dtype)

    score, finish = _attn_stages(n_chunks, n_chains, s_bufs, m_bufs, load_q, key_chunk, val_chunk,
                                 lambda i, ch, c: None, emit)
    _two_stage(n_tiles, score, finish)


def _split_pair(q2, first):
    zero = jnp.zeros_like(q2)
    return jnp.concatenate([jnp.where(first, q2, zero), jnp.where(first, zero, q2)], axis=0)


def _score_scratch(n_chains, rows, n_keys):
    return ([pltpu.VMEM((rows, n_keys), F32) for _ in range(2 * n_chains)]
            + [pltpu.VMEM((rows, 1), F32) for _ in range(2 * n_chains)])


def _split_scratch(scratch, n_chains):
    s_flat, m_flat = scratch[:2 * n_chains], scratch[2 * n_chains:]
    return ([s_flat[2 * ch:2 * ch + 2] for ch in range(n_chains)],
            [m_flat[2 * ch:2 * ch + 2] for ch in range(n_chains)])


def _attention(q, kc, vc, kl, vl, *, groups, q_lanes, k_lanes, heads, mode, tq, name):
    b, l, _ = q.shape
    c = kc.shape[1]
    has_latent = kl is not None
    tq = min(tq, l)
    n_keys = c + (kl.shape[1] if has_latent else 0)
    o_lanes = heads * LANES if mode == "shared" else LANES

    def gspec(n, w):
        return pl.BlockSpec((None, n, w), lambda bb, g: (bb, 0, g))

    in_specs = [gspec(l, q_lanes), gspec(c, k_lanes), gspec(c, LANES)]
    args = [q, kc, vc]
    if has_latent:
        s = kl.shape[1]
        in_specs += [gspec(s, k_lanes), gspec(s, LANES)]
        args += [kl, vl]
    return pl.pallas_call(
        functools.partial(_attn_kernel, tq=tq, heads=heads, mode=mode, has_latent=has_latent),
        out_shape=jax.ShapeDtypeStruct((b, l, groups * o_lanes), BF16),
        grid=(b, groups),
        in_specs=in_specs,
        out_specs=gspec(l, o_lanes),
        scratch_shapes=(_score_scratch(2, tq, n_keys) if mode == "slots"
                        else _score_scratch(1, heads * tq, n_keys)),
        compiler_params=_cparams("parallel", "parallel"),
        name=name,
    )(*args)


def _na_kernel(q_ref, kc_ref, vc_ref, kl_ref, vl_ref, bias_ref, o_ref, *scratch, n_blocks):
    tq = NA_QROWS * GRID_W
    band = NA_BAND * GRID_W
    rows_total = n_blocks * NA_QROWS
    s_bufs, m_bufs = _split_scratch(scratch, 1)
    ctx_chunks = kc_ref.shape[0] // KEY_CHUNK
    n_chunks = ctx_chunks + band // KEY_CHUNK
    lane = lax.broadcasted_iota(jnp.int32, (1, LANES), 1)
    first = lane < (LANES // 2)

    def rows_of(i):
        start = i * tq
        return pl.ds(start if isinstance(start, int) else pl.multiple_of(start, tq), tq)

    def band_chunk(i, c):
        if isinstance(i, int):
            row0 = min(max(i * NA_QROWS - NA_WIN_H // 2, 0), rows_total - NA_BAND)
            return pl.ds(row0 * GRID_W + c * KEY_CHUNK, KEY_CHUNK)
        row0 = jnp.clip(i * NA_QROWS - NA_WIN_H // 2, 0, rows_total - NA_BAND)
        return pl.ds(pl.multiple_of(row0 * GRID_W + c * KEY_CHUNK, GRID_W), KEY_CHUNK)

    def variant_of(i):
        if isinstance(i, int):
            return 0 if i == 0 else (2 if i == n_blocks - 1 else 1)
        return jnp.where(i == 0, 0, jnp.where(i == n_blocks - 1, 2, 1))

    def load_q(i, ch):
        return _split_pair(q_ref[rows_of(i), :], first)

    def key_chunk(i, ch, c):
        if c < ctx_chunks:
            return kc_ref[c * KEY_CHUNK:(c + 1) * KEY_CHUNK, :]
        return kl_ref[band_chunk(i, c - ctx_chunks), :]

    def val_chunk(i, ch, c):
        if c < ctx_chunks:
            return vc_ref[c * KEY_CHUNK:(c + 1) * KEY_CHUNK, :]
        return vl_ref[band_chunk(i, c - ctx_chunks), :]

    def bias_chunk(i, ch, c):
        if c < ctx_chunks:
            return None
        cols = slice((c - ctx_chunks) * KEY_CHUNK, (c - ctx_chunks + 1) * KEY_CHUNK)
        var = variant_of(i)
        return jnp.concatenate([bias_ref[0, var, :, cols], bias_ref[1, var, :, cols]], axis=0)

    def emit(i, outs):
        o_ref[rows_of(i), :] = jnp.where(first, outs[0][:tq], outs[0][tq:]).astype(o_ref.dtype)

    score, finish = _attn_stages(n_chunks, 1, s_bufs, m_bufs, load_q, key_chunk, val_chunk, bias_chunk, emit)
    _two_stage(n_blocks, score, finish)


def _na_attention(q, kc, vc, kl, vl, bias):
    b, s, w = q.shape
    c = kc.shape[1]
    pairs = w // LANES
    n_blocks = s // (NA_QROWS * GRID_W)

    def gspec(n):
        return pl.BlockSpec((None, n, LANES), lambda g, bb: (bb, 0, g))

    return pl.pallas_call(
        functools.partial(_na_kernel, n_blocks=n_blocks),
        out_shape=jax.ShapeDtypeStruct((b, s, w), BF16),
        grid=(pairs, b),
        in_specs=[gspec(s), gspec(c), gspec(c), gspec(s), gspec(s),
                  pl.BlockSpec((2,) + bias.shape[1:], lambda g, bb: (g, 0, 0, 0))],
        out_specs=gspec(s),
        scratch_shapes=_score_scratch(1, 2 * NA_QROWS * GRID_W, c + NA_BAND * GRID_W),
        compiler_params=_cparams("parallel", "parallel"),
        name="na_attention",
    )(q, kc, vc, kl, vl, bias)


def _na_bias_table(rpb, rows):
    n_blocks = rows // NA_QROWS
    n_heads, n_rel = rpb.shape[0], 2 * NA_WIN_H - 1
    j = jnp.arange(GRID_W)
    c0 = jnp.clip(j - NA_WIN_W // 2, 0, GRID_W - NA_WIN_W)
    col_ok = (j[None, :] >= c0[:, None]) & (j[None, :] < c0[:, None] + NA_WIN_W)
    col_idx = jnp.clip(j[None, :] - j[:, None] + NA_WIN_W - 1, 0, 2 * NA_WIN_W - 2)
    blocks = jnp.where(col_ok[None, None], rpb.astype(F32)[:, :, col_idx] * LOG2E, NEG_INF)
    pad_lo, pad_hi = NA_QROWS, NA_BAND
    strip = jnp.pad(blocks.transpose(0, 2, 1, 3), ((0, 0), (0, 0), (pad_lo, pad_hi), (0, 0)),
                    constant_values=NEG_INF)
    key_row = jnp.arange(NA_BAND)
    tables = []
    for i in (0, 1, n_blocks - 1):
        band0 = min(max(i * NA_QROWS - NA_WIN_H // 2, 0), rows - NA_BAND)
        per_row = []
        for a in range(NA_QROWS):
            qr = i * NA_QROWS + a
            r0 = min(max(qr - NA_WIN_H // 2, 0), rows - NA_WIN_H)
            rel0 = band0 - qr + NA_WIN_H - 1 + pad_lo
            assert 0 <= rel0 and rel0 + NA_BAND <= n_rel + pad_lo + pad_hi
            window = strip[:, :, rel0:rel0 + NA_BAND, :]
            row_ok = (band0 + key_row >= r0) & (band0 + key_row < r0 + NA_WIN_H)
            per_row.append(jnp.where(row_ok[None, None, :, None], window, NEG_INF))
        tables.append(jnp.stack(per_row, axis=1).reshape(n_heads, NA_QROWS * GRID_W, NA_BAND * GRID_W))
    return jnp.stack(tables, axis=1)


def _s5_kernel(uc_ref, ul_ref, kc_ref, ws_ref, wo_ref, dec_ref, yc_ref, yl_ref, s_ref, h_ref, *, batch):
    half = 2 * S5_STATE
    lane = lax.broadcasted_iota(jnp.int32, (1, half), 1)
    fwd = lane < S5_STATE
    dec = dec_ref[...]
    d_re, d_im = dec[:, :half], dec[:, half:]

    def scan(u_ref, y_ref, n_chunks, init_re, init_im):
        rows = n_chunks * batch
        u = u_ref[...]
        s_ref[0:rows, :] = _dot(u, ws_ref[...])

        def step(i, carry):
            st_re, st_im = carry
            rf = pl.ds(pl.multiple_of(i * batch, batch), batch)
            rb = pl.ds(pl.multiple_of((n_chunks - 1 - i) * batch, batch), batch)
            h_ref[rf, 0:S5_STATE] = st_re[:, :S5_STATE]
            h_ref[rf, half:half + S5_STATE] = st_im[:, :S5_STATE]
            h_ref[rb, S5_STATE:half] = st_re[:, S5_STATE:]
            h_ref[rb, half + S5_STATE:] = st_im[:, S5_STATE:]
            in_re = jnp.where(fwd, s_ref[rf, 0:half], s_ref[rb, 0:half])
            in_im = jnp.where(fwd, s_ref[rf, half:], s_ref[rb, half:])
            return (d_re * st_re - d_im * st_im + in_re, d_re * st_im + d_im * st_re + in_im)

        st_re, st_im = lax.fori_loop(0, n_chunks, step, (init_re, init_im))
        y = _dot(u, kc_ref[...]) + _dot(h_ref[0:rows, :].astype(BF16), wo_ref[...])
        y_ref[...] = y.astype(y_ref.dtype)
        return st_re, st_im

    zero = jnp.zeros((batch, half), F32)
    ctx_re, ctx_im = scan(uc_ref, yc_ref, uc_ref.shape[0] // batch, zero, zero)
    scan(ul_ref, yl_ref, ul_ref.shape[0] // batch, ctx_re, ctx_im)


def _s5_scan(uc, ul, kc, ws, wo, dec, batch):
    groups, rows_c, width = uc.shape
    rows_l = ul.shape[1]

    def gspec(n, w):
        return pl.BlockSpec((None, n, w), lambda g: (g, 0, 0))

    return pl.pallas_call(
        functools.partial(_s5_kernel, batch=batch),
        out_shape=(jax.ShapeDtypeStruct((groups, rows_c, width), BF16),
                   jax.ShapeDtypeStruct((groups, rows_l, width), BF16)),
        grid=(groups,),
        in_specs=[gspec(rows_c, width), gspec(rows_l, width), gspec(width, width), gspec(width, width),
                  gspec(width, width), gspec(1, width)],
        out_specs=(gspec(rows_c, width), gspec(rows_l, width)),
        scratch_shapes=[pltpu.VMEM((rows_l, width), F32), pltpu.VMEM((rows_l, width), F32)],
        compiler_params=_cparams("parallel"),
        name="s5_scan",
    )(uc, ul, kc, ws, wo, dec)


def _s5_matrices(a_re, a_im, log_dt, b_re, b_im, c_re, c_im):
    t = S5_CHUNK
    dt = jnp.exp(log_dt.astype(F32))[..., None]
    a_re, a_im = a_re.astype(F32), a_im.astype(F32)
    lam_re, lam_im = dt * a_re, dt * a_im

    def power(j):
        jj = jnp.asarray(j, F32)
        mag = jnp.exp(lam_re[..., None] * jj)
        return mag * jnp.cos(lam_im[..., None] * jj), mag * jnp.sin(lam_im[..., None] * jj)

    ab_re, ab_im = power(jnp.ones((1,)))
    ab_re, ab_im = ab_re[..., 0], ab_im[..., 0]
    den = a_re * a_re + a_im * a_im
    nr = ab_re - 1.0
    f_re = (nr * a_re + ab_im * a_im) / den
    f_im = (ab_im * a_re - nr * a_im) / den
    bb_re = f_re[..., None] * b_re - f_im[..., None] * b_im
    bb_im = f_re[..., None] * b_im + f_im[..., None] * b_re
    c_re, c_im = c_re.astype(F32), c_im.astype(F32)

    hi = lax.Precision.HIGHEST
    lags = jnp.arange(t)
    pw_re, pw_im = power(lags)
    cr, ci = c_re[:, :, :, None, :], c_im[:, :, :, None, :]
    pr, pi = pw_re.transpose(0, 1, 3, 2)[:, :, None], pw_im.transpose(0, 1, 3, 2)[:, :, None]
    cb_re = cr * pr - ci * pi
    cb_im = cr * pi + ci * pr
    conv = (jnp.einsum('dgclp,dgpi->dglic', cb_re, bb_re, precision=hi)
            - jnp.einsum('dgclp,dgpi->dglic', cb_im, bb_im, precision=hi))
    src, dst = lags[:, None], lags[None, :]
    lag_f = dst - src
    k_f = jnp.where((lag_f >= 0)[None, :, None, :, None], conv[0][:, jnp.clip(lag_f, 0, t - 1)].transpose(0, 1, 3, 2, 4), 0.0)
    lag_b = src - dst
    k_b = jnp.where((lag_b >= 0)[None, :, None, :, None], conv[1][:, jnp.clip(lag_b, 0, t - 1)].transpose(0, 1, 3, 2, 4), 0.0)
    g = a_re.shape[1]
    cg = b_re.shape[-1]
    k_mat = (k_f + k_b).reshape(g, t * cg, t * cg)

    def state_w(d, expo):
        p_re, p_im = power(expo)
        w_re = p_re[d][..., None] * bb_re[d][:, :, None, :] - p_im[d][..., None] * bb_im[d][:, :, None, :]
        w_im = p_re[d][..., None] * bb_im[d][:, :, None, :] + p_im[d][..., None] * bb_re[d][:, :, None, :]
        to_rows = lambda w: w.transpose(0, 2, 3, 1).reshape(g, t * cg, -1)
        return to_rows(w_re), to_rows(w_im)

    wf_re, wf_im = state_w(0, t - 1 - lags)
    wb_re, wb_im = state_w(1, lags)
    w_state = jnp.concatenate([wf_re, wb_re, wf_im, wb_im], axis=-1)

    def out_w(d, expo):
        p_re, p_im = power(expo)
        m_re = c_re[d][:, :, :, None] * p_re[d][:, None, :, :] - c_im[d][:, :, :, None] * p_im[d][:, None, :, :]
        m_im = c_re[d][:, :, :, None] * p_im[d][:, None, :, :] + c_im[d][:, :, :, None] * p_re[d][:, None, :, :]
        to_cols = lambda w: w.transpose(0, 2, 3, 1).reshape(g, -1, t * cg)
        return to_cols(m_re), -to_cols(m_im)

    of_re, of_im = out_w(0, lags + 1)
    ob_re, ob_im = out_w(1, t - lags)
    w_out = jnp.concatenate([of_re, ob_re, of_im, ob_im], axis=1)

    dT_re, dT_im = power(jnp.full((1,), float(t)))
    dT_re, dT_im = dT_re[..., 0], dT_im[..., 0]
    decay = jnp.concatenate([dT_re[0], dT_re[1], dT_im[0], dT_im[1]], axis=-1)[:, None, :]
    return k_mat.astype(BF16), w_state.astype(BF16), w_out.astype(BF16), decay


def _s5_proj_kernel(x_ref, sh_ref, sc_ref, g_ref, u_ref):
    u_ref[...] = _modulate(x_ref[...], g_ref[...], sh_ref[...], sc_ref[...]).astype(BF16)


def _s5_proj(x, mod, is_ctx, norm_g, tm):
    b, l, d = x.shape
    return pl.pallas_call(
        _s5_proj_kernel,
        out_shape=jax.ShapeDtypeStruct((b, l, d), BF16),
        grid=(b, l // tm),
        in_specs=[_tok_spec(tm, d), mod.spec(0, is_ctx, 0), mod.spec(1, is_ctx, 0), _const_spec((1, d))],
        out_specs=_tok_spec(tm, d),
        compiler_params=_cparams("parallel", "parallel"),
        name="s5_proj",
    )(x, mod.table, mod.table, norm_g)


def _to_chunks(u):
    b, l, d = u.shape
    g = d // S5_GROUP
    u = u.reshape(b, l // S5_CHUNK, S5_CHUNK, g, S5_GROUP).transpose(3, 1, 0, 2, 4)
    return u.reshape(g, (l // S5_CHUNK) * b, S5_CHUNK * S5_GROUP)


def _from_chunks(y, b):
    g, rows, _ = y.shape
    n = rows // b
    y = y.reshape(g, n, b, S5_CHUNK, S5_GROUP).transpose(2, 1, 3, 0, 4)
    return y.reshape(b, n * S5_CHUNK, g * S5_GROUP)


def _post_ffn_kernel(x_ref, y_ref, m_ref, gmix_ref, gffn_ref, dskip_ref, w_mix_ref, w_in_ref, w_out_ref,
                     o_ref, acc_ref, *, glu, hidden, chunk):
    sh, sc, gt, sh2, sc2, gt2 = (m_ref[i:i + 1, :] for i in range(6))
    x = x_ref[...]
    if glu:
        yv = dskip_ref[...] * _modulate(x, gmix_ref[...], sh, sc) + y_ref[...].astype(F32)
        ab = _dot(_gelu_tanh(yv).astype(BF16), w_mix_ref[...])
        n = ab.shape[-1] // 2
        mix = ab[:, :n] * jax.nn.sigmoid(ab[:, n:])
    else:
        mix = _dot(y_ref[...], w_mix_ref[...])
    x = x + gt * mix
    h = _modulate(x, gffn_ref[...], sh2, sc2).astype(BF16)
    for c0 in range(0, hidden, chunk):
        a = _dot(h, w_in_ref[:, c0:c0 + chunk])
        b = _dot(h, w_in_ref[:, hidden + c0:hidden + c0 + chunk])
        part = _dot((_silu(a) * b).astype(BF16), w_out_ref[c0:c0 + chunk, :])
        if c0 == 0:
            acc_ref[...] = part
        else:
            acc_ref[...] += part
    o_ref[...] = x + gt2 * acc_ref[...]


def _post_ffn(x, y, mod, is_ctx, g_mix, g_ffn, d_skip, w_mix, w_in, w_out, glu, tm):
    b, l, d = x.shape
    hidden = w_out.shape[0]
    ctx_row = mod.ctx_row
    if is_ctx:
        mspec = pl.BlockSpec((None, 6, d), lambda bb, j: (ctx_row, 0, 0))
    else:
        mspec = pl.BlockSpec((None, 6, d), lambda bb, j: (bb, 0, 0))
    single = dict(pipeline_mode=pl.Buffered(1))

    def wspec(shape):
        nd = len(shape)
        return pl.BlockSpec(shape, lambda *ids: (0,) * nd, **single)

    return pl.pallas_call(
        functools.partial(_post_ffn_kernel, glu=glu, hidden=hidden, chunk=hidden // 2),
        out_shape=jax.ShapeDtypeStruct((b, l, d), F32),
        grid=(b, l // tm),
        in_specs=[_tok_spec(tm, d), _tok_spec(tm, y.shape[-1]), mspec, _const_spec((1, d)), _const_spec((1, d)),
                  _const_spec((1, d)), wspec(w_mix.shape), wspec(w_in.shape), wspec(w_out.shape)],
        out_specs=_tok_spec(tm, d),
        scratch_shapes=[pltpu.VMEM((tm, d), F32)],
        compiler_params=_cparams("parallel", "parallel"),
        name="post_ffn_ctx" if is_ctx else "post_ffn",
    )(x, y, mod.table.reshape(mod.table.shape[0], 6, d), g_mix, g_ffn, d_skip, w_mix, w_in, w_out)


def _rope_tables(n_tokens, rot_dim, offset, width):
    t = jnp.arange(n_tokens, dtype=jnp.int32)
    axis_dim = rot_dim // 2
    inv_freq = ROPE_THETA ** (-jnp.arange(0, axis_dim, 2, dtype=F32) / axis_dim)
    lane = jnp.arange(width)
    j = lane - offset
    in_rot = (j >= 0) & (j < rot_dim)
    jc = jnp.clip(j, 0, rot_dim - 1)
    use_col = jc >= axis_dim
    within = jc % axis_dim
    freq = inv_freq[within % (axis_dim // 2)]
    lower = within < (axis_dim // 2)
    pos = jnp.where(use_col[None, :], (t % GRID_W)[:, None], (t // GRID_W)[:, None]).astype(F32)
    ang = pos * freq[None, :]
    cos = jnp.where(in_rot[None, :], jnp.cos(ang), 1.0)
    sin = jnp.where(in_rot[None, :], jnp.where(lower[None, :], -jnp.sin(ang), jnp.sin(ang)), 0.0)
    return cos, sin


def _mla_weights(w_in, g_q, g_kv, w_uq, w_ukv, g_qn, g_kn):
    d = w_in.shape[0]
    nh, slot = MLA_HEADS, MLA_SLOT
    rope0 = MLA_Q_LORA + MLA_KV_LORA
    w_rope = w_in[:, rope0:]
    quarter = MLA_ROPE // 4
    partner = jnp.arange(MLA_ROPE).reshape(-1, 2, quarter)[:, ::-1, :].reshape(-1)

    def slotted(cols):
        return jnp.pad(cols, ((0, 0), (MLA_NOPE, slot - MLA_QK)))

    w_in_p = jnp.concatenate([w_in[:, :rope0], slotted(w_rope), slotted(w_rope[:, partner])], axis=1)
    w_uq_p = jnp.pad(w_uq.reshape(-1, nh, MLA_QK), ((0, 0), (0, 0), (0, slot - MLA_QK))).reshape(-1, nh * slot)
    kv = w_ukv.reshape(-1, nh, MLA_NOPE + MLA_V)
    w_k = jnp.pad(kv[:, :, :MLA_NOPE], ((0, 0), (0, 0), (0, slot - MLA_NOPE))).reshape(-1, nh * slot)
    w_v = kv[:, :, MLA_NOPE:].reshape(-1, nh * MLA_V)
    w_ukv_p = jnp.concatenate([w_k, w_v], axis=1)
    pad_slot = lambda g: jnp.pad(g, (0, slot - g.shape[0]))[None, :]
    gqn = pad_slot(g_qn)
    gkn_nope = pad_slot(g_kn[:MLA_NOPE])
    gkr = jnp.pad(g_kn[MLA_NOPE:], (MLA_NOPE, slot - MLA_QK))[None, :]
    gkrs = jnp.pad(g_kn[MLA_NOPE:][partner], (MLA_NOPE, slot - MLA_QK))[None, :]
    return (w_in_p.astype(BF16), g_q[None, :], g_kv[None, :], w_uq_p.astype(BF16), w_ukv_p.astype(BF16),
            gqn, gkn_nope, gkr, gkrs)


def _tile_gain(g):
    return jnp.tile(g, LANES // g.shape[0])[None, :]


def kernel(x, c, ctx, c_ctx, ada_w, ada_b, norm_mix, norm_ffn, ffn_w_in, ffn_w_out,
           mla_w_in, mla_g_q, mla_g_kv, mla_w_uq, mla_w_ukv, mla_g_qn, mla_g_kn, mla_w_o,
           s5_a_re, s5_a_im, s5_log_dt, s5_b_re, s5_b_im, s5_c_re, s5_c_im, s5_d, s5_w_glu,
           na_w_qkv, na_g_qn, na_g_kn, na_rpb, na_w_o,
           gqa_w_qkv, gqa_g_qn, gqa_g_kn, gqa_w_o):
    batch, seq, d = x.shape
    n_ctx = ctx.shape[1]
    depth = ada_w.shape[0]
    n_mixers = 4
    ctx_row = batch
    mod_rows = -(-(batch + 1) // 8) * 8
    cvec = jnp.concatenate([c, c_ctx[None, :], jnp.zeros((mod_rows - batch - 1, d), F32)], axis=0)
    ada = _ada_terms(cvec, ada_w, ada_b).reshape(depth, mod_rows, 1, 6 * d)

    mla_tabs = _rope_tables(seq, MLA_ROPE, MLA_NOPE, MLA_SLOT)
    gqa_dh = d // GQA_HEADS
    gqa_tabs = _rope_tables(seq, gqa_dh, 0, LANES)
    ident_tabs = (jnp.ones((n_ctx, LANES), F32), jnp.zeros((n_ctx, LANES), F32))
    ident_tabs_l = (jnp.ones((seq, LANES), F32), jnp.zeros((seq, LANES), F32))
    zero_skip = jnp.zeros((1, d), F32)
    tm_l, tm_c = 512, 256

    xl, xc = x, ctx
    for i in range(depth):
        kind, j = i % n_mixers, i // n_mixers
        ctx_out = i < depth - 1
        mod = _Mod(ada[i], d, ctx_row)
        g_mix, g_ffn = norm_mix[i][None, :], norm_ffn[i][None, :]
        glu = False
        d_skip = zero_skip
        if kind == 0:
            wts = _mla_weights(mla_w_in[j], mla_g_q[j], mla_g_kv[j], mla_w_uq[j], mla_w_ukv[j],
                               mla_g_qn[j], mla_g_kn[j])
            qc, kc, vc = _mla_proj(xc, mod, True, g_mix, ident_tabs, wts, tm_c, need_q=ctx_out)
            ql, kl, vl = _mla_proj(xl, mod, False, g_mix, mla_tabs, wts, tm_c)
            cfg = dict(groups=MLA_HEADS // 2, q_lanes=2 * MLA_SLOT, k_lanes=2 * MLA_SLOT, heads=2,
                       mode="slots", tq=256)
            yl = _attention(ql, kc, vc, kl, vl, name="mla_attn", **cfg)
            yc = _attention(qc, kc, vc, None, None, name="mla_attn_ctx", **cfg) if ctx_out else None
            w_mix = mla_w_o[j].astype(BF16)
        elif kind == 1:
            uc = _to_chunks(_s5_proj(xc, mod, True, g_mix, tm_c))
            ul = _to_chunks(_s5_proj(xl, mod, False, g_mix, tm_l))
            mats = _s5_matrices(s5_a_re[j], s5_a_im[j], s5_log_dt[j], s5_b_re[j], s5_b_im[j],
                                s5_c_re[j], s5_c_im[j])
            yc, yl = _s5_scan(uc, ul, *mats, batch)
            yc, yl = _from_chunks(yc, batch), _from_chunks(yl, batch)
            w_mix = s5_w_glu[j].astype(BF16)
            glu = True
            d_skip = s5_d[j][None, :]
        elif kind == 2:
            dh = d // NA_HEADS
            w = na_w_qkv[j].astype(BF16)
            gq, gk = _tile_gain(na_g_qn[j]), _tile_gain(na_g_kn[j])
            qc, kc, vc = _qkv_proj(xc, mod, True, g_mix, ident_tabs, w, gq, gk, NA_HEADS, NA_HEADS, dh, tm_c,
                                   False, ctx_out, "na_proj_ctx")
            ql, kl, vl = _qkv_proj(xl, mod, False, g_mix, ident_tabs_l, w, gq, gk, NA_HEADS, NA_HEADS, dh, tm_l,
                                   False, True, "na_proj")
            bias = _na_bias_table(na_rpb[j], seq // GRID_W)
            yl = _na_attention(ql, kc, vc, kl, vl, bias)
            if ctx_out:
                yc = _attention(qc, kc, vc, None, None, groups=NA_HEADS // 2, q_lanes=LANES, k_lanes=LANES,
                                heads=2, mode="pair", tq=256, name="na_attn_ctx")
            w_mix = na_w_o[j].astype(BF16)
        else:
            w = gqa_w_qkv[j].astype(BF16)
            gq, gk = gqa_g_qn[j][None, :], gqa_g_kn[j][None, :]
            per_kv = GQA_HEADS // GQA_KV_HEADS
            qc, kc, vc = _qkv_proj(xc, mod, True, g_mix, ident_tabs, w, gq, gk, GQA_HEADS, GQA_KV_HEADS, gqa_dh,
                                   tm_c, False, ctx_out, "gqa_proj_ctx")
            ql, kl, vl = _qkv_proj(xl, mod, False, g_mix, gqa_tabs, w, gq, gk, GQA_HEADS, GQA_KV_HEADS, gqa_dh,
                                   tm_l, True, True, "gqa_proj")
            cfg = dict(groups=GQA_KV_HEADS, q_lanes=per_kv * LANES, k_lanes=LANES, heads=per_kv, mode="shared",
                       tq=128)
            yl = _attention(ql, kc, vc, kl, vl, name="gqa_attn", **cfg)
            yc = _attention(qc, kc, vc, None, None, name="gqa_attn_ctx", **cfg) if ctx_out else None
            w_mix = gqa_w_o[j].astype(BF16)

        w_in, w_out = ffn_w_in[i].astype(BF16), ffn_w_out[i].astype(BF16)
        xl = _post_ffn(xl, yl, mod, False, g_mix, g_ffn, d_skip, w_mix, w_in, w_out, glu, tm_l)
        if ctx_out:
            xc = _post_ffn(xc, yc, mod, True, g_mix, g_ffn, d_skip, w_mix, w_in, w_out, glu, tm_c)
    return xl
```
